```python
import math
import jax, jax.numpy as jnp
from jax import lax
import numpy as np

D_MODEL = 2048
BATCH = 8
SEQ = 4096
DEPTH = 4
DEC_BATCH = 8
DEC_SEQ = 32
PAST_LEN = 4096

CHUNK = 64
N_META = 16
N_MIXERS = 2
N_SSM_LAYERS = (DEPTH + 1) // 2
N_SB_LAYERS = DEPTH // 2
SSM_GROUP_CH = 16
SSM_GROUPS = D_MODEL // SSM_GROUP_CH
SSM_STATE = 64
SB_HEADS = 16
SB_HEAD_DIM = D_MODEL // SB_HEADS
Q_BLOCK = 128
D_FF = 4 * D_MODEL
NORM_EPS = 1e-6
DT_MIN = 1e-3
DT_MAX = 1e-1

kernel_name = "s5_stickbreaking_hybrid_stream_step"

F32 = jnp.float32


def _rms_norm(x, gain):
    xf = x.astype(F32)
    y = xf * lax.rsqrt(jnp.mean(xf * xf, axis=-1, keepdims=True) + NORM_EPS)
    return (y * gain.astype(F32)).astype(x.dtype)


def _mlp(x, w_up, w_down):
    h = jnp.square(jax.nn.relu(x @ w_up))
    return h @ w_down


def _ssm_discretise(a_re, a_im, log_dt, b_re, b_im):
    a_re, a_im, b_re, b_im = (t.astype(F32) for t in (a_re, a_im, b_re, b_im))
    dt = jnp.exp(log_dt.astype(F32))[:, None]
    mag = jnp.exp(a_re * dt)
    ang = a_im * dt
    ab_re = mag * jnp.cos(ang)
    ab_im = mag * jnp.sin(ang)
    den = a_re * a_re + a_im * a_im
    nr = ab_re - 1.0
    f_re = (nr * a_re + ab_im * a_im) / den
    f_im = (ab_im * a_re - nr * a_im) / den
    bb_re = f_re[..., None] * b_re - f_im[..., None] * b_im
    bb_im = f_re[..., None] * b_im + f_im[..., None] * b_re
    return ab_re, ab_im, bb_re, bb_im


def _ssm_combine(early, late):
    a1r, a1i, b1r, b1i = early
    a2r, a2i, b2r, b2i = late
    return (a2r * a1r - a2i * a1i,
            a2r * a1i + a2i * a1r,
            a2r * b1r - a2i * b1i + b2r,
            a2r * b1i + a2i * b1r + b2i)


def _ssm_block(carry, u_blk, ab_re, ab_im, bb_re, bb_im, c_re, c_im):
    L, B, _ = u_blk.shape
    ug = u_blk.reshape(L, B, SSM_GROUPS, SSM_GROUP_CH)
    bu_re = jnp.einsum('lbgc,gpc->lbgp', ug, bb_re)
    bu_im = jnp.einsum('lbgc,gpc->lbgp', ug, bb_im)
    a_re = jnp.broadcast_to(ab_re, bu_re.shape)
    a_im = jnp.broadcast_to(ab_im, bu_re.shape)
    ca_re, ca_im, cb_re, cb_im = lax.associative_scan(
        _ssm_combine, (a_re, a_im, bu_re, bu_im), axis=0)
    s0_re, s0_im = carry
    s_re = ca_re * s0_re - ca_im * s0_im + cb_re
    s_im = ca_re * s0_im + ca_im * s0_re + cb_im
    y = (jnp.einsum('lbgp,gcp->lbgc', s_re, c_re)
         - jnp.einsum('lbgp,gcp->lbgc', s_im, c_im))
    return (s_re[-1], s_im[-1]), y.reshape(L, B, D_MODEL)


def _ssm_prompt(u, disc, c_re, c_im):
    B, T, _ = u.shape
    pad = (-T) % CHUNK
    up = jnp.pad(u, ((0, 0), (pad, 0), (0, 0)))
    nb = (T + pad) // CHUNK
    blocks = up.reshape(B, nb, CHUNK, D_MODEL).transpose(1, 2, 0, 3)
    s0 = (jnp.zeros((B, SSM_GROUPS, SSM_STATE), F32), jnp.zeros((B, SSM_GROUPS, SSM_STATE), F32))
    (s_re, s_im), ys = lax.scan(
        lambda c, ub: _ssm_block(c, ub, *disc, c_re, c_im), s0, blocks)
    y = ys.transpose(2, 0, 1, 3).reshape(B, nb * CHUNK, D_MODEL)[:, pad:]
    return y, s_re, s_im


def _ssm_output(y, u, d_skip, w_glu, dtype):
    g = jax.nn.gelu(y + d_skip.astype(F32) * u)
    return (g * jax.nn.sigmoid(g @ w_glu.astype(F32))).astype(dtype)


def _sb_project(h, w_qkv, q_gain, k_gain):
    B, T, _ = h.shape
    qkv = (h @ w_qkv).reshape(B, T, 3, SB_HEADS, SB_HEAD_DIM)
    q = _rms_norm(qkv[:, :, 0], q_gain)
    k = _rms_norm(qkv[:, :, 1], k_gain)
    v = qkv[:, :, 2]
    return q, k, v


def _sb_attend(q, k, v, q_pos, k_pos):
    z = jnp.einsum('bqhd,bkhd->bhqk', q.astype(F32), k.astype(F32)) * (SB_HEAD_DIM ** -0.5)
    visible = k_pos[None, :] < q_pos[:, None]
    log_keep = jnp.where(visible, jax.nn.log_sigmoid(-z), 0.0)
    log_after = lax.cumsum(log_keep, axis=3, reverse=True) - log_keep
    w = jnp.where(visible, jnp.exp(jax.nn.log_sigmoid(z) + log_after), 0.0)
    return jnp.einsum('bhqk,bkhd->bqhd', w, v.astype(F32)).astype(q.dtype)


def _sb_prompt(q, k, v):
    B, T, H, Dh = q.shape
    pad = (-T) % Q_BLOCK
    widths = ((0, 0), (0, pad), (0, 0), (0, 0))
    qp, kp, vp = jnp.pad(q, widths), jnp.pad(k, widths), jnp.pad(v, widths)
    Tp = T + pad
    nb = Tp // Q_BLOCK
    pos = jnp.arange(Tp, dtype=jnp.int32)
    qb = qp.reshape(B, nb, Q_BLOCK, H, Dh).transpose(1, 0, 2, 3, 4)
    pb = pos.reshape(nb, Q_BLOCK)
    out = lax.map(lambda xs: _sb_attend(xs[0], kp, vp, xs[1], pos), (qb, pb))
    return out.transpose(1, 0, 2, 3, 4).reshape(B, Tp, H, Dh)[:, :T]


def setup_inputs(seed: int = 0) -> dict:
    key = jax.random.key(seed)
    ks = jax.random.split(key, 24)
    nrm = jax.random.normal
    G, P, C = SSM_GROUPS, SSM_STATE, SSM_GROUP_CH
    a_im = jnp.broadcast_to(math.pi * jnp.arange(P, dtype=F32), (N_SSM_LAYERS, G, P))
    return {
        "x_prompt": nrm(ks[0], (BATCH, SEQ, D_MODEL), F32),
        "x_sample": nrm(ks[1], (DEC_BATCH, DEC_SEQ, D_MODEL), F32),
        "state_ssm_re": 0.5 * nrm(ks[2], (N_SSM_LAYERS, DEC_BATCH, G, P), F32),
        "state_ssm_im": 0.5 * nrm(ks[3], (N_SSM_LAYERS, DEC_BATCH, G, P), F32),
        "cache_k": nrm(ks[4], (N_SB_LAYERS, DEC_BATCH, PAST_LEN, SB_HEADS, SB_HEAD_DIM), F32),
        "cache_v": nrm(ks[5], (N_SB_LAYERS, DEC_BATCH, PAST_LEN, SB_HEADS, SB_HEAD_DIM), F32),
        "meta_tokens": nrm(ks[6], (N_META, D_MODEL), F32),
        "norm_mix": 1.0 + 0.02 * nrm(ks[7], (DEPTH, D_MODEL), F32),
        "norm_mlp": 1.0 + 0.02 * nrm(ks[8], (DEPTH, D_MODEL), F32),
        "ssm_a_re": -0.5 + 0.01 * nrm(ks[9], (N_SSM_LAYERS, G, P), F32),
        "ssm_a_im": a_im + 0.0 * nrm(ks[10], (N_SSM_LAYERS, G, P), F32) + 0.01 * nrm(ks[10], (N_SSM_LAYERS, G, P), F32),
        "ssm_log_dt": jax.random.uniform(ks[11], (N_SSM_LAYERS, G), F32, math.log(DT_MIN), math.log(DT_MAX)),
        "ssm_b_re": nrm(ks[12], (N_SSM_LAYERS, G, P, C), F32) * (2.0 * C) ** -0.5,
        "ssm_b_im": nrm(ks[13], (N_SSM_LAYERS, G, P, C), F32) * (2.0 * C) ** -0.5,
        "ssm_c_re": nrm(ks[14], (N_SSM_LAYERS, G, C, P), F32) * (2.0 * P) ** -0.5,
        "ssm_c_im": nrm(ks[15], (N_SSM_LAYERS, G, C, P), F32) * (2.0 * P) ** -0.5,
        "ssm_d": nrm(ks[16], (N_SSM_LAYERS, D_MODEL), F32),
        "ssm_w_glu": nrm(ks[17], (N_SSM_LAYERS, D_MODEL, D_MODEL), F32) * D_MODEL ** -0.5,
        "sb_w_qkv": nrm(ks[18], (N_SB_LAYERS, D_MODEL, 3 * D_MODEL), F32) * D_MODEL ** -0.5,
        "sb_q_norm": 1.0 + 0.02 * nrm(ks[19], (N_SB_LAYERS, SB_HEAD_DIM), F32),
        "sb_k_norm": 1.0 + 0.02 * nrm(ks[20], (N_SB_LAYERS, SB_HEAD_DIM), F32),
        "sb_w_o": nrm(ks[21], (N_SB_LAYERS, D_MODEL, D_MODEL), F32) * D_MODEL ** -0.5,
        "mlp_w_up": nrm(ks[22], (DEPTH, D_MODEL, D_FF), F32) * D_MODEL ** -0.5,
        "mlp_w_down": nrm(ks[23], (DEPTH, D_FF, D_MODEL), F32) * D_FF ** -0.5,
    }


def reference(x_prompt, x_sample, state_ssm_re, state_ssm_im, cache_k, cache_v,
              meta_tokens, norm_mix, norm_mlp,
              ssm_a_re, ssm_a_im, ssm_log_dt, ssm_b_re, ssm_b_im, ssm_c_re, ssm_c_im,
              ssm_d, ssm_w_glu,
              sb_w_qkv, sb_q_norm, sb_k_norm, sb_w_o,
              mlp_w_up, mlp_w_down):
    B = x_prompt.shape[0]
    DB, DS = x_sample.shape[0], x_sample.shape[1]
    past = cache_k.shape[2]
    meta = jnp.broadcast_to(meta_tokens[None].astype(x_prompt.dtype), (B, N_META, D_MODEL))
    h_p = jnp.concatenate([meta, x_prompt], axis=1)
    h_s = x_sample

    k_pos_s = jnp.arange(N_META + past + DS, dtype=jnp.int32)
    q_pos_s = N_META + past + jnp.arange(DS, dtype=jnp.int32)

    ssm_re_p, ssm_im_p, ssm_re_s, ssm_im_s = [], [], [], []
    k_p_rows, v_p_rows, k_s_rows, v_s_rows = [], [], [], []
    i_ssm, i_sb = 0, 0
    for layer in range(DEPTH):
        u_p = _rms_norm(h_p, norm_mix[layer])
        u_s = _rms_norm(h_s, norm_mix[layer])
        if layer % N_MIXERS == 0:
            disc = _ssm_discretise(ssm_a_re[i_ssm], ssm_a_im[i_ssm], ssm_log_dt[i_ssm],
                                   ssm_b_re[i_ssm], ssm_b_im[i_ssm])
            c_re = ssm_c_re[i_ssm].astype(F32)
            c_im = ssm_c_im[i_ssm].astype(F32)
            up32 = u_p.astype(F32)
            y_p, sre_p, sim_p = _ssm_prompt(up32, disc, c_re, c_im)
            us32 = u_s.astype(F32)
            carry = (state_ssm_re[i_ssm].astype(F32), state_ssm_im[i_ssm].astype(F32))
            (sre_s, sim_s), y_s = _ssm_block(carry, us32.transpose(1, 0, 2), *disc, c_re, c_im)
            y_s = y_s.transpose(1, 0, 2)
            h_p = h_p + _ssm_output(y_p, up32, ssm_d[i_ssm], ssm_w_glu[i_ssm], h_p.dtype)
            h_s = h_s + _ssm_output(y_s, us32, ssm_d[i_ssm], ssm_w_glu[i_ssm], h_s.dtype)
            ssm_re_p.append(sre_p)
            ssm_im_p.append(sim_p)
            ssm_re_s.append(sre_s)
            ssm_im_s.append(sim_s)
            i_ssm += 1
        else:
            q_p, k_p, v_p = _sb_project(u_p, sb_w_qkv[i_sb], sb_q_norm[i_sb], sb_k_norm[i_sb])
            o_p = _sb_prompt(q_p, k_p, v_p)
            h_p = h_p + o_p.reshape(B, -1, D_MODEL) @ sb_w_o[i_sb]
            q_s, k_s, v_s = _sb_project(u_s, sb_w_qkv[i_sb], sb_q_norm[i_sb], sb_k_norm[i_sb])
            meta_k = jnp.broadcast_to(k_p[:1, :N_META], (DB, N_META, SB_HEADS, SB_HEAD_DIM))
            meta_v = jnp.broadcast_to(v_p[:1, :N_META], (DB, N_META, SB_HEADS, SB_HEAD_DIM))
            k_all = jnp.concatenate([meta_k.astype(k_s.dtype), cache_k[i_sb].astype(k_s.dtype), k_s], axis=1)
            v_all = jnp.concatenate([meta_v.astype(v_s.dtype), cache_v[i_sb].astype(v_s.dtype), v_s], axis=1)
            o_s = _sb_attend(q_s, k_all, v_all, q_pos_s, k_pos_s)
            h_s = h_s + o_s.reshape(DB, DS, D_MODEL) @ sb_w_o[i_sb]
            k_p_rows.append(k_p)
            v_p_rows.append(v_p)
            k_s_rows.append(k_s)
            v_s_rows.append(v_s)
            i_sb += 1
        h_p = h_p + _mlp(_rms_norm(h_p, norm_mlp[layer]), mlp_w_up[layer], mlp_w_down[layer])
        h_s = h_s + _mlp(_rms_norm(h_s, norm_mlp[layer]), mlp_w_up[layer], mlp_w_down[layer])

    y_prompt = h_p[:, N_META:]
    y_sample = h_s
    new_ssm_re_p = jnp.stack(ssm_re_p)
    new_ssm_im_p = jnp.stack(ssm_im_p)
    new_k_p = jnp.stack(k_p_rows)
    new_v_p = jnp.stack(v_p_rows)
    new_ssm_re_s = jnp.stack(ssm_re_s)
    new_ssm_im_s = jnp.stack(ssm_im_s)
    new_k_s = jnp.stack(k_s_rows)
    new_v_s = jnp.stack(v_s_rows)
    return (y_prompt, y_sample, new_ssm_re_p, new_ssm_im_p, new_k_p, new_v_p,
            new_ssm_re_s, new_ssm_im_s, new_k_s, new_v_s)
```

```python
import functools
import math

import jax
import jax.numpy as jnp
from jax import lax
from jax.experimental import pallas as pl
from jax.experimental.pallas import tpu as pltpu

F32 = jnp.float32
BF16 = jnp.bfloat16

NORM_EPS = 1e-6
HEAD_DIM = 128
GROUP_CH = 16
SSM_STATE = 64
SSM_CHUNK = 32
GROUPS_PER_STEP = 2
ATT_TK = 128
ROW_ALIGN = 128
VMEM_LIMIT_BYTES = 56 * 1024 * 1024


def _cparams(*sem):
    return pltpu.CompilerParams(dimension_semantics=sem, vmem_limit_bytes=VMEM_LIMIT_BYTES)


def _pick(n, candidates):
    for c in candidates:
        if n % c == 0:
            return c
    raise ValueError(f"no tile in {candidates} divides {n}")


def _dot(a, b):
    return jnp.dot(a, b, preferred_element_type=F32)


def _dot_nt(a, b):
    return lax.dot_general(a, b, (((1,), (1,)), ((), ())), preferred_element_type=F32)


def _split_bf16(x):
    hi = x.astype(BF16)
    lo = (x - hi.astype(F32)).astype(BF16)
    return hi, lo


def _rms(x, gain):
    ms = jnp.mean(x * x, axis=-1, keepdims=True)
    return x * lax.rsqrt(ms + NORM_EPS) * gain


def _rmsnorm_kernel(x_ref, g_ref, o_ref):
    o_ref[...] = _rms(x_ref[...], g_ref[...]).astype(o_ref.dtype)


def _rmsnorm(x, gain, out_dtype):
    m, d = x.shape
    tm = _pick(m, (512, 256, 128))
    return pl.pallas_call(
        _rmsnorm_kernel,
        grid=(m // tm,),
        in_specs=[pl.BlockSpec((tm, d), lambda i: (i, 0)),
                  pl.BlockSpec((1, d), lambda i: (0, 0))],
        out_specs=pl.BlockSpec((tm, d), lambda i: (i, 0)),
        out_shape=jax.ShapeDtypeStruct((m, d), out_dtype),
        compiler_params=_cparams("parallel"),
        name="rmsnorm",
    )(x, gain.reshape(1, d))


def _norm_matmul_kernel(x_ref, g_ref, w_ref, *rest, epilogue, norm_tiles):
    if epilogue == "qkv":
        hg_ref, o_ref, xn_ref = rest
    else:
        o_ref, xn_ref = rest
    j = pl.program_id(1)

    @pl.when(j == 0)
    def _():
        xn_ref[...] = _rms(x_ref[...], g_ref[...]).astype(BF16)

    acc = _dot(xn_ref[...], w_ref[...])
    if epilogue == "relu2":
        o_ref[...] = jnp.square(jnp.maximum(acc, 0.0)).astype(o_ref.dtype)
    else:
        @pl.when(j < norm_tiles)
        def _():
            for hh in range(acc.shape[1] // HEAD_DIM):
                seg = acc[:, hh * HEAD_DIM:(hh + 1) * HEAD_DIM]
                o_ref[:, hh * HEAD_DIM:(hh + 1) * HEAD_DIM] = _rms(seg, hg_ref[0])

        @pl.when(j >= norm_tiles)
        def _():
            o_ref[...] = acc


def _norm_matmul(x, gain, w, epilogue, out_dtype, head_gains=None):
    m, d = x.shape
    n = w.shape[1]
    tm = _pick(m, (1024, 512, 256, 128))
    tn = _pick(d, (1024, 512, 256, 128)) if epilogue == "qkv" else _pick(n, (1024, 512, 256, 128))
    in_specs = [pl.BlockSpec((tm, d), lambda i, j: (i, 0)),
                pl.BlockSpec((1, d), lambda i, j: (0, 0)),
                pl.BlockSpec((d, tn), lambda i, j: (0, j))]
    args = [x, gain.reshape(1, d), w]
    norm_tiles = 0
    if epilogue == "qkv":
        tiles_per_part = d // tn
        norm_tiles = 2 * tiles_per_part
        in_specs.append(pl.BlockSpec((1, 1, HEAD_DIM), lambda i, j: (j // tiles_per_part, 0, 0)))
        args.append(head_gains)
    return pl.pallas_call(
        functools.partial(_norm_matmul_kernel, epilogue=epilogue, norm_tiles=norm_tiles),
        grid=(m // tm, n // tn),
        in_specs=in_specs,
        out_specs=pl.BlockSpec((tm, tn), lambda i, j: (i, j)),
        out_shape=jax.ShapeDtypeStruct((m, n), out_dtype),
        scratch_shapes=[pltpu.VMEM((tm, d), BF16)],
        compiler_params=_cparams("parallel", "arbitrary"),
        name="norm_matmul_" + epilogue,
    )(*args)


def _matmul_resid_kernel(x_ref, w_ref, r_ref, o_ref):
    @pl.when(pl.program_id(2) == 0)
    def _():
        o_ref[...] = r_ref[...]

    o_ref[...] += _dot(x_ref[...], w_ref[...])


def _matmul_resid(x, w, resid):
    m, k = x.shape
    n = w.shape[1]
    tm = _pick(m, (1024, 512, 256, 128))
    tn = _pick(n, (1024, 512, 256, 128))
    tk = _pick(k, (2048, 1024, 512, 256, 128))
    return pl.pallas_call(
        _matmul_resid_kernel,
        grid=(m // tm, n // tn, k // tk),
        in_specs=[pl.BlockSpec((tm, tk), lambda i, j, kk: (i, kk)),
                  pl.BlockSpec((tk, tn), lambda i, j, kk: (kk, j)),
                  pl.BlockSpec((tm, tn), lambda i, j, kk: (i, j))],
        out_specs=pl.BlockSpec((tm, tn), lambda i, j, kk: (i, j)),
        out_shape=jax.ShapeDtypeStruct((m, n), F32),
        compiler_params=_cparams("parallel", "parallel", "arbitrary"),
        name="matmul_resid",
    )(x, w, resid)


def _glu_kernel(y_ref, u_ref, h_ref, d_ref, w_ref, o_ref):
    g = jax.nn.gelu(y_ref[...] + d_ref[...] * u_ref[...])
    gate = jax.nn.sigmoid(_dot(g.astype(BF16), w_ref[...]))
    o_ref[...] = h_ref[...] + g * gate


def _glu(y, u, h, d_skip, w):
    m, d = y.shape
    tm = _pick(m, (256, 128))
    row = pl.BlockSpec((tm, d), lambda i: (i, 0))
    return pl.pallas_call(
        _glu_kernel,
        grid=(m // tm,),
        in_specs=[row, row, row,
                  pl.BlockSpec((1, d), lambda i: (0, 0)),
                  pl.BlockSpec((d, d), lambda i: (0, 0))],
        out_specs=row,
        out_shape=jax.ShapeDtypeStruct((m, d), F32),
        compiler_params=_cparams("parallel"),
        name="ssm_glu",
    )(y, u, h, d_skip.reshape(1, d), w)


def _ssm_kernel(x_ref, are_ref, aim_ref, ldt_ref, btr_ref, bti_ref, cr_ref, ci_ref, s0r_ref, s0i_ref,
                y_ref, sfr_ref, sfi_ref,
                pr_ref, pi_ref, g_ref, w2_ref, e2_ref, m_ref, v_ref, s_ref, *, nchunks, batch):
    L = SSM_CHUNK
    C = GROUP_CH
    LC = L * C
    P2 = GROUPS_PER_STEP * SSM_STATE

    are = are_ref[0]
    aim = aim_ref[0]
    dt = jnp.exp(ldt_ref[0])
    mag = jnp.exp(are * dt)
    ang = aim * dt
    abr = mag * jnp.cos(ang)
    abi = mag * jnp.sin(ang)
    den = are * are + aim * aim
    nr = abr - 1.0
    f_re = (nr * are + abi * aim) / den
    f_im = (abi * are - nr * aim) / den

    pr = jnp.ones_like(abr)
    pi_ = jnp.zeros_like(abr)
    for k in range(L + 1):
        pr_ref[k:k + 1, :] = pr
        pi_ref[k:k + 1, :] = pi_
        pr, pi_ = pr * abr - pi_ * abi, pr * abi + pi_ * abr

    lane = lax.broadcasted_iota(jnp.int32, (C, LC), 1)
    for gl in range(GROUPS_PER_STEP):
        btr = btr_ref[0, gl]
        bti = bti_ref[0, gl]
        bbr = f_re * btr - f_im * bti
        bbi = f_re * bti + f_im * btr
        cr = cr_ref[0, gl]
        ci = ci_ref[0, gl]
        for k in range(L + 1):
            ar = pr_ref[k:k + 1, :]
            ai = pi_ref[k:k + 1, :]
            g_ref[gl, k * C:(k + 1) * C, 0:P2] = cr * ar - ci * ai
            g_ref[gl, k * C:(k + 1) * C, P2:2 * P2] = -(cr * ai + ci * ar)
            if k < L:
                r0 = (gl * L + (L - 1 - k)) * C
                w2_ref[r0:r0 + C, 0:P2] = (bbr * ar - bbi * ai).astype(BF16)
                w2_ref[r0:r0 + C, P2:2 * P2] = (bbr * ai + bbi * ar).astype(BF16)
        bb_hi, bb_lo = _split_bf16(jnp.concatenate([bbr, bbi], axis=1))
        gt_hi, gt_lo = _split_bf16(g_ref[gl, 0:LC, :])
        krow = _dot_nt(bb_hi, gt_hi) + _dot_nt(bb_hi, gt_lo) + _dot_nt(bb_lo, gt_hi)
        for s in range(L):
            slab = krow if s == 0 else jnp.where(lane >= s * C, pltpu.roll(krow, s * C, 1), 0.0)
            m_ref[gl, s * C:(s + 1) * C, :] = slab.astype(BF16)
        e2_ref[gl * LC:(gl + 1) * LC, :] = g_ref[gl, C:LC + C, :].astype(BF16)

    v_ref[...] = _dot(x_ref[0], w2_ref[0:LC, :]) + _dot(x_ref[1], w2_ref[LC:2 * LC, :])

    alr = jnp.broadcast_to(pr_ref[L:L + 1, :], (batch, P2))
    ali = jnp.broadcast_to(pi_ref[L:L + 1, :], (batch, P2))

    def chunk_step(n, carry):
        sr, si = carry
        r0 = pl.multiple_of(n * batch, batch)
        s_ref[pl.ds(r0, batch), 0:P2] = sr
        s_ref[pl.ds(r0, batch), P2:2 * P2] = si
        vr = v_ref[pl.ds(r0, batch), 0:P2]
        vi = v_ref[pl.ds(r0, batch), P2:2 * P2]
        return sr * alr - si * ali + vr, sr * ali + si * alr + vi

    sr, si = lax.fori_loop(0, nchunks, chunk_step, (s0r_ref[0], s0i_ref[0]))
    sfr_ref[0] = sr
    sfi_ref[0] = si

    sb = s_ref[...].astype(BF16)
    for gl in range(GROUPS_PER_STEP):
        y_ref[gl] = _dot(x_ref[gl], m_ref[gl]) + _dot_nt(sb, e2_ref[gl * LC:(gl + 1) * LC, :])


def _pair_lanes(p):
    g, pp = p.shape
    return p.reshape(g // GROUPS_PER_STEP, 1, GROUPS_PER_STEP * pp)


def _pair_embed(t):
    g, c, pp = t.shape
    t = t.reshape(g // 2, 2, c, pp)
    z = jnp.zeros_like(t[:, 0])
    first = jnp.concatenate([t[:, 0], z], axis=-1)
    second = jnp.concatenate([z, t[:, 1]], axis=-1)
    return jnp.stack([first, second], axis=1)


def _ssm_params(a_re, a_im, log_dt, b_re, b_im, c_re, c_im):
    p = a_re.shape[1]
    return (_pair_lanes(a_re), _pair_lanes(a_im),
            _pair_lanes(jnp.broadcast_to(log_dt[:, None], (log_dt.shape[0], p))),
            _pair_embed(jnp.swapaxes(b_re, 1, 2)), _pair_embed(jnp.swapaxes(b_im, 1, 2)),
            _pair_embed(c_re), _pair_embed(c_im))


def _ssm_scan(u, params, s0_re, s0_im):
    b, t, d = u.shape
    g = d // GROUP_CH
    L = SSM_CHUNK
    LC = L * GROUP_CH
    P2 = GROUPS_PER_STEP * SSM_STATE
    nchunks = t // L
    rows = nchunks * b
    npairs = g // GROUPS_PER_STEP
    x = u.astype(BF16).reshape(b, nchunks, L, g, GROUP_CH).transpose(3, 1, 0, 2, 4).reshape(g, rows, LC)
    s0r = s0_re.reshape(b, npairs, P2).transpose(1, 0, 2)
    s0i = s0_im.reshape(b, npairs, P2).transpose(1, 0, 2)
    vec = pl.BlockSpec((1, 1, P2), lambda i: (i, 0, 0))
    emb = pl.BlockSpec((1, 2, GROUP_CH, P2), lambda i: (i, 0, 0, 0))
    st = pl.BlockSpec((1, b, P2), lambda i: (i, 0, 0))
    xy = pl.BlockSpec((2, rows, LC), lambda i: (i, 0, 0))
    y, sfr, sfi = pl.pallas_call(
        functools.partial(_ssm_kernel, nchunks=nchunks, batch=b),
        grid=(npairs,),
        in_specs=[xy, vec, vec, vec, emb, emb, emb, emb, st, st],
        out_specs=[xy, st, st],
        out_shape=[jax.ShapeDtypeStruct((g, rows, LC), F32),
                   jax.ShapeDtypeStruct((npairs, b, P2), F32),
                   jax.ShapeDtypeStruct((npairs, b, P2), F32)],
        scratch_shapes=[pltpu.VMEM((L + 8, P2), F32),
                        pltpu.VMEM((L + 8, P2), F32),
                        pltpu.VMEM((2, LC + GROUP_CH, 2 * P2), F32),
                        pltpu.VMEM((2 * LC, 2 * P2), BF16),
                        pltpu.VMEM((2 * LC, 2 * P2), BF16),
                        pltpu.VMEM((2, LC, LC), BF16),
                        pltpu.VMEM((rows, 2 * P2), F32),
                        pltpu.VMEM((rows, 2 * P2), F32)],
        compiler_params=_cparams("parallel"),
        name="ssm_scan",
    )(x, *params, s0r, s0i)
    y = y.reshape(g, nchunks, b, L, GROUP_CH).transpose(2, 1, 3, 0, 4).reshape(b, t, d)
    sf_re = sfr.transpose(1, 0, 2).reshape(b, g, SSM_STATE)
    sf_im = sfi.transpose(1, 0, 2).reshape(b, g, SSM_STATE)
    return y, sf_re, sf_im


def _tri_ones(tk):
    j = jnp.arange(tk)[:, None]
    s = jnp.arange(tk)[None, :]
    return jnp.concatenate([(j > s).astype(BF16), jnp.ones((tk, HEAD_DIM), BF16)], axis=1)


def _sb_block(q, k, v, c, tri, mask):
    tq = q.shape[0]
    tk = k.shape[0]
    z = _dot_nt(q, k) * (HEAD_DIM ** -0.5)
    t = jnp.log1p(jnp.exp(-jnp.abs(z)))
    log_keep = -jnp.maximum(z, 0.0) - t
    log_beta = jnp.minimum(z, 0.0) - t
    if mask is not None:
        log_keep = jnp.where(mask, log_keep, 0.0)
    hi, lo = _split_bf16(log_keep)
    s2 = _dot(jnp.concatenate([hi, lo], axis=0), tri)
    s = s2[:tq] + s2[tq:]
    after = s[:, :tk]
    row_sum = s[:, tk:]
    cb = c if tk == HEAD_DIM else jnp.concatenate([c] * (tk // HEAD_DIM), axis=1)
    w = jnp.exp(log_beta + after + cb)
    if mask is not None:
        w = jnp.where(mask, w, 0.0)
    return _dot(w.astype(BF16), v), row_sum


def _attn_prompt_kernel(q_ref, k_ref, v_ref, tri_ref, o_ref, kb_ref, vb_ref, *, tq, pad):
    t_len = q_ref.shape[1]
    tk = ATT_TK
    ratio = tq // tk
    kb_ref[...] = k_ref[0].astype(BF16)
    vb_ref[...] = v_ref[0].astype(BF16)
    tri = tri_ref[...]

    def q_step(qi, carry):
        q0 = pl.multiple_of(qi * tq, tq)
        q = q_ref[0, pl.ds(q0, tq), :].astype(BF16)
        qpos = q0 + lax.broadcasted_iota(jnp.int32, (tq, tk), 0)

        def k_step(step, kc):
            acc, c = kc
            k0 = pl.multiple_of(((qi + 1) * ratio - 1 - step) * tk, tk)
            kpos = k0 + lax.broadcasted_iota(jnp.int32, (tq, tk), 1)
            mask = jnp.logical_and(kpos < qpos, kpos >= pad)
            pv, rs = _sb_block(q, kb_ref[pl.ds(k0, tk), :], vb_ref[pl.ds(k0, tk), :], c, tri, mask)
            return acc + pv, c + rs

        zero = jnp.zeros((tq, HEAD_DIM), F32)
        acc, _ = lax.fori_loop(0, (qi + 1) * ratio, k_step, (zero, zero))
        o_ref[0, pl.ds(q0, tq), :] = acc.astype(o_ref.dtype)
        return carry

    lax.fori_loop(0, t_len // tq, q_step, 0)


def _attn_prompt(qkv, pad):
    b, t, d3 = qkv.shape
    d = d3 // 3
    nh = d // HEAD_DIM
    tq = _pick(t, (384, 256, 128))
    col = lambda off: pl.BlockSpec((1, t, HEAD_DIM), lambda bi, hi: (bi, 0, off + hi))
    return pl.pallas_call(
        functools.partial(_attn_prompt_kernel, tq=tq, pad=pad),
        grid=(b, nh),
        in_specs=[col(0), col(nh), col(2 * nh),
                  pl.BlockSpec((ATT_TK, ATT_TK + HEAD_DIM), lambda bi, hi: (0, 0))],
        out_specs=pl.BlockSpec((1, t, HEAD_DIM), lambda bi, hi: (bi, 0, hi)),
        out_shape=jax.ShapeDtypeStruct((b, t, d), BF16),
        scratch_shapes=[pltpu.VMEM((t, HEAD_DIM), BF16), pltpu.VMEM((t, HEAD_DIM), BF16)],
        compiler_params=_cparams("parallel", "parallel"),
        name="attn_prompt",
    )(qkv, qkv, qkv, _tri_ones(ATT_TK))


def _attn_sample_kernel(q_ref, kn_ref, vn_ref, ck_ref, cv_ref, km_ref, vm_ref, tri_ref, o_ref, *, n_meta):
    tq = q_ref.shape[1]
    tk = ATT_TK
    past = ck_ref.shape[1]
    tri = tri_ref[...]
    q = q_ref[0].astype(BF16)
    qidx = lax.broadcasted_iota(jnp.int32, (tq, tk), 0)
    kidx = lax.broadcasted_iota(jnp.int32, (tq, tk), 1)
    zero = jnp.zeros((tq, HEAD_DIM), F32)

    acc, c = _sb_block(q, kn_ref[0].astype(BF16), vn_ref[0].astype(BF16), zero, tri, kidx < qidx)

    def k_step(step, kc):
        acc, c = kc
        k0 = pl.multiple_of(past - (step + 1) * tk, tk)
        pv, rs = _sb_block(q, ck_ref[0, pl.ds(k0, tk), :].astype(BF16),
                           cv_ref[0, pl.ds(k0, tk), :].astype(BF16), c, tri, None)
        return acc + pv, c + rs

    acc, c = lax.fori_loop(0, past // tk, k_step, (acc, c))
    pv, _ = _sb_block(q, km_ref[...].astype(BF16), vm_ref[...].astype(BF16), c, tri, kidx < n_meta)
    o_ref[0] = (acc + pv).astype(o_ref.dtype)


def _attn_sample(qkv, cache_k, cache_v, meta_k, meta_v):
    b, s, d3 = qkv.shape
    d = d3 // 3
    nh = d // HEAD_DIM
    past = cache_k.shape[1]
    n_meta = meta_k.shape[0]
    assert s <= ATT_TK and n_meta <= ATT_TK and past % ATT_TK == 0
    k_new = jnp.pad(qkv[:, :, d:2 * d], ((0, 0), (0, ATT_TK - s), (0, 0)))
    v_new = jnp.pad(qkv[:, :, 2 * d:], ((0, 0), (0, ATT_TK - s), (0, 0)))
    meta_k = jnp.pad(meta_k, ((0, ATT_TK - n_meta), (0, 0)))
    meta_v = jnp.pad(meta_v, ((0, ATT_TK - n_meta), (0, 0)))
    head = lambda rows: pl.BlockSpec((1, rows, HEAD_DIM), lambda bi, hi: (bi, 0, hi))
    meta = pl.BlockSpec((ATT_TK, HEAD_DIM), lambda bi, hi: (0, hi))
    return pl.pallas_call(
        functools.partial(_attn_sample_kernel, n_meta=n_meta),
        grid=(b, nh),
        in_specs=[head(s), head(ATT_TK), head(ATT_TK), head(past), head(past), meta, meta,
                  pl.BlockSpec((ATT_TK, ATT_TK + HEAD_DIM), lambda bi, hi: (0, 0))],
        out_specs=head(s),
        out_shape=jax.ShapeDtypeStruct((b, s, d), BF16),
        compiler_params=_cparams("parallel", "parallel"),
        name="attn_sample",
    )(qkv, k_new, v_new, cache_k, cache_v, meta_k, meta_v, _tri_ones(ATT_TK))


def kernel(x_prompt, x_sample, state_ssm_re, state_ssm_im, cache_k, cache_v, meta_tokens, norm_mix, norm_mlp, ssm_a_re, ssm_a_im, ssm_log_dt, ssm_b_re, ssm_b_im, ssm_c_re, ssm_c_im, ssm_d, ssm_w_glu, sb_w_qkv, sb_q_norm, sb_k_norm, sb_w_o, mlp_w_up, mlp_w_down):
    b, seq, d = x_prompt.shape
    db, ds, _ = x_sample.shape
    depth = norm_mix.shape[0]
    n_meta = meta_tokens.shape[0]
    nh = d // HEAD_DIM
    past = cache_k.shape[2]
    assert ds == SSM_CHUNK, "the running streams advance by one S5 chunk per step"
    t_real = n_meta + seq
    t_len = -(-t_real // ROW_ALIGN) * ROW_ALIGN
    pad = t_len - t_real

    meta = jnp.broadcast_to(meta_tokens[None].astype(F32), (b, n_meta, d))
    h_p = jnp.concatenate([jnp.zeros((b, pad, d), F32), meta, x_prompt], axis=1).reshape(b * t_len, d)
    h_s = x_sample.reshape(db * ds, d)
    zero_state = jnp.zeros((b, d // GROUP_CH, SSM_STATE), F32)

    outs = {name: [] for name in ("re_p", "im_p", "k_p", "v_p", "re_s", "im_s", "k_s", "v_s")}
    i_ssm = i_sb = 0
    for layer in range(depth):
        if layer % 2 == 0:
            params = _ssm_params(ssm_a_re[i_ssm], ssm_a_im[i_ssm], ssm_log_dt[i_ssm], ssm_b_re[i_ssm],
                                 ssm_b_im[i_ssm], ssm_c_re[i_ssm], ssm_c_im[i_ssm])
            w_glu = ssm_w_glu[i_ssm].astype(BF16)
            u_p = _rmsnorm(h_p, norm_mix[layer], F32)
            y_p, re_p, im_p = _ssm_scan(u_p.reshape(b, t_len, d), params, zero_state, zero_state)
            h_p = _glu(y_p.reshape(b * t_len, d), u_p, h_p, ssm_d[i_ssm], w_glu)
            u_s = _rmsnorm(h_s, norm_mix[layer], F32)
            y_s, re_s, im_s = _ssm_scan(u_s.reshape(db, ds, d), params, state_ssm_re[i_ssm], state_ssm_im[i_ssm])
            h_s = _glu(y_s.reshape(db * ds, d), u_s, h_s, ssm_d[i_ssm], w_glu)
            outs["re_p"].append(re_p)
            outs["im_p"].append(im_p)
            outs["re_s"].append(re_s)
            outs["im_s"].append(im_s)
            i_ssm += 1
        else:
            w_qkv = sb_w_qkv[i_sb].astype(BF16)
            w_o = sb_w_o[i_sb].astype(BF16)
            gains = jnp.stack([sb_q_norm[i_sb], sb_k_norm[i_sb], jnp.ones_like(sb_q_norm[i_sb])]).reshape(3, 1, HEAD_DIM)
            qkv_p = _norm_matmul(h_p, norm_mix[layer], w_qkv, "qkv", F32, gains).reshape(b, t_len, 3 * d)
            o_p = _attn_prompt(qkv_p, pad)
            h_p = _matmul_resid(o_p.reshape(b * t_len, d), w_o, h_p)
            qkv_s = _norm_matmul(h_s, norm_mix[layer], w_qkv, "qkv", F32, gains).reshape(db, ds, 3 * d)
            o_s = _attn_sample(qkv_s, cache_k[i_sb].reshape(db, past, d), cache_v[i_sb].reshape(db, past, d),
                               qkv_p[0, pad:pad + n_meta, d:2 * d], qkv_p[0, pad:pad + n_meta, 2 * d:])
            h_s = _matmul_resid(o_s.reshape(db * ds, d), w_o, h_s)
            outs["k_p"].append(qkv_p[:, pad:, d:2 * d].reshape(b, t_real, nh, HEAD_DIM))
            outs["v_p"].append(qkv_p[:, pad:, 2 * d:].reshape(b, t_real, nh, HEAD_DIM))
            outs["k_s"].append(qkv_s[:, :, d:2 * d].reshape(db, ds, nh, HEAD_DIM))
            outs["v_s"].append(qkv_s[:, :, 2 * d:].reshape(db, ds, nh, HEAD_DIM))
            i_sb += 1
        w_up = mlp_w_up[layer].astype(BF16)
        w_down = mlp_w_down[layer].astype(BF16)
        h_p = _matmul_resid(_norm_matmul(h_p, norm_mlp[layer], w_up, "relu2", BF16), w_down, h_p)
        h_s = _matmul_resid(_norm_matmul(h_s, norm_mlp[layer], w_up, "relu2", BF16), w_down, h_s)

    y_prompt = h_p.reshape(b, t_len, d)[:, pad + n_meta:]
    y_sample = h_s.reshape(db, ds, d)
    st = {k: jnp.stack(v) for k, v in outs.items()}
    return (y_prompt, y_sample, st["re_p"], st["im_p"], st["k_p"], st["v_p"],
            st["re_s"], st["im_s"], st["k_s"], st["v_s"])
```

```python
import functools
import math

import jax
import jax.numpy as jnp
from jax import lax
from jax.experimental import pallas as pl
from jax.experimental.pallas import tpu as pltpu

F32 = jnp.float32
BF16 = jnp.bfloat16

NORM_EPS = 1e-6
HEAD_DIM = 128
GROUP_CH = 16
SSM_STATE = 64
SSM_CHUNK = 16
SSM_BLOCK_GROUPS = 8
SSM_TILE_CHUNKS = 88
ATT_TK = 128
ATT_WINDOW = 3
SKIP_LOG = -100.0
ROW_ALIGN = 128
VMEM_LIMIT_BYTES = 56 * 1024 * 1024


def _cparams(*sem):
    return pltpu.CompilerParams(dimension_semantics=sem, vmem_limit_bytes=VMEM_LIMIT_BYTES)


def _pick(n, candidates):
    for c in candidates:
        if n % c == 0:
            return c
    raise ValueError(f"no tile in {candidates} divides {n}")


def _dot(a, b):
    return jnp.dot(a, b, preferred_element_type=F32)


def _dot_nt(a, b):
    return lax.dot_general(a, b, (((1,), (1,)), ((), ())), preferred_element_type=F32)


def _split_bf16(x):
    hi = x.astype(BF16)
    lo = (x - hi.astype(F32)).astype(BF16)
    return hi, lo


def _rms(x, gain):
    ms = jnp.mean(x * x, axis=-1, keepdims=True)
    return x * lax.rsqrt(ms + NORM_EPS) * gain


def _rmsnorm_kernel(x_ref, g_ref, o_ref):
    o_ref[...] = _rms(x_ref[...], g_ref[...]).astype(o_ref.dtype)


def _rmsnorm(x, gain, out_dtype):
    m, d = x.shape
    tm = _pick(m, (512, 256, 128))
    return pl.pallas_call(
        _rmsnorm_kernel,
        grid=(m // tm,),
        in_specs=[pl.BlockSpec((tm, d), lambda i: (i, 0)),
                  pl.BlockSpec((1, d), lambda i: (0, 0))],
        out_specs=pl.BlockSpec((tm, d), lambda i: (i, 0)),
        out_shape=jax.ShapeDtypeStruct((m, d), out_dtype),
        compiler_params=_cparams("parallel"),
        name="rmsnorm",
    )(x, gain.reshape(1, d))


def _norm_matmul_kernel(x_ref, g_ref, w_ref, *rest, epilogue, norm_tiles):
    if epilogue == "qkv":
        hg_ref, o_ref, xn_ref = rest
    else:
        o_ref, xn_ref = rest
    j = pl.program_id(1)

    @pl.when(j == 0)
    def _():
        xn_ref[...] = _rms(x_ref[...], g_ref[...]).astype(BF16)

    acc = _dot(xn_ref[...], w_ref[...])
    if epilogue == "relu2":
        o_ref[...] = jnp.square(jnp.maximum(acc, 0.0)).astype(o_ref.dtype)
    else:
        @pl.when(j < norm_tiles)
        def _():
            for hh in range(acc.shape[1] // HEAD_DIM):
                seg = acc[:, hh * HEAD_DIM:(hh + 1) * HEAD_DIM]
                o_ref[:, hh * HEAD_DIM:(hh + 1) * HEAD_DIM] = _rms(seg, hg_ref[0])

        @pl.when(j >= norm_tiles)
        def _():
            o_ref[...] = acc


def _norm_matmul(x, gain, w, epilogue, out_dtype, head_gains=None):
    m, d = x.shape
    n = w.shape[1]
    tm = _pick(m, (1024, 512, 256, 128))
    tn = _pick(d, (1024, 512, 256, 128)) if epilogue == "qkv" else _pick(n, (1024, 512, 256, 128))
    in_specs = [pl.BlockSpec((tm, d), lambda i, j: (i, 0)),
                pl.BlockSpec((1, d), lambda i, j: (0, 0)),
                pl.BlockSpec((d, tn), lambda i, j: (0, j))]
    args = [x, gain.reshape(1, d), w]
    norm_tiles = 0
    if epilogue == "qkv":
        tiles_per_part = d // tn
        norm_tiles = 2 * tiles_per_part
        in_specs.append(pl.BlockSpec((1, 1, HEAD_DIM), lambda i, j: (j // tiles_per_part, 0, 0)))
        args.append(head_gains)
    return pl.pallas_call(
        functools.partial(_norm_matmul_kernel, epilogue=epilogue, norm_tiles=norm_tiles),
        grid=(m // tm, n // tn),
        in_specs=in_specs,
        out_specs=pl.BlockSpec((tm, tn), lambda i, j: (i, j)),
        out_shape=jax.ShapeDtypeStruct((m, n), out_dtype),
        scratch_shapes=[pltpu.VMEM((tm, d), BF16)],
        compiler_params=_cparams("parallel", "arbitrary"),
        name="norm_matmul_" + epilogue,
    )(*args)


def _matmul_resid_kernel(x_ref, w_ref, r_ref, o_ref):
    @pl.when(pl.program_id(2) == 0)
    def _():
        o_ref[...] = r_ref[...]

    o_ref[...] += _dot(x_ref[...], w_ref[...])


def _matmul_resid(x, w, resid):
    m, k = x.shape
    n = w.shape[1]
    tm = _pick(m, (1024, 512, 256, 128))
    tn = _pick(n, (1024, 512, 256, 128))
    tk = _pick(k, (2048, 1024, 512, 256, 128))
    return pl.pallas_call(
        _matmul_resid_kernel,
        grid=(m // tm, n // tn, k // tk),
        in_specs=[pl.BlockSpec((tm, tk), lambda i, j, kk: (i, kk)),
                  pl.BlockSpec((tk, tn), lambda i, j, kk: (kk, j)),
                  pl.BlockSpec((tm, tn), lambda i, j, kk: (i, j))],
        out_specs=pl.BlockSpec((tm, tn), lambda i, j, kk: (i, j)),
        out_shape=jax.ShapeDtypeStruct((m, n), F32),
        compiler_params=_cparams("parallel", "parallel", "arbitrary"),
        name="matmul_resid",
    )(x, w, resid)


def _glu_kernel(y_ref, u_ref, h_ref, d_ref, w_ref, o_ref):
    g = jax.nn.gelu(y_ref[...] + d_ref[...] * u_ref[...])
    gate = jax.nn.sigmoid(_dot(g.astype(BF16), w_ref[...]))
    o_ref[...] = h_ref[...] + g * gate


def _glu(y, u, h, d_skip, w):
    m, d = y.shape
    tm = _pick(m, (256, 128))
    row = pl.BlockSpec((tm, d), lambda i: (i, 0))
    return pl.pallas_call(
        _glu_kernel,
        grid=(m // tm,),
        in_specs=[row, row, row,
                  pl.BlockSpec((1, d), lambda i: (0, 0)),
                  pl.BlockSpec((d, d), lambda i: (0, 0))],
        out_specs=row,
        out_shape=jax.ShapeDtypeStruct((m, d), F32),
        compiler_params=_cparams("parallel"),
        name="ssm_glu",
    )(y, u, h, d_skip.reshape(1, d), w)


def _ssm_kernel(x_ref, are_ref, aim_ref, ldt_ref, btr_ref, bti_ref, cr_ref, ci_ref, s0r_ref, s0i_ref,
                y_ref, sfr_ref, sfi_ref,
                pr_ref, pi_ref, tw_ref, wst_ref, et_ref, lhs_ref, v_ref, s_ref, cr_state, ci_state,
                *, nch_t, batch, final_chunk):
    L = SSM_CHUNK
    W = SSM_BLOCK_GROUPS * GROUP_CH
    PS = SSM_BLOCK_GROUPS * SSM_STATE
    nt = pl.program_id(1)

    @pl.when(nt == 0)
    def _():
        are = are_ref[0]
        aim = aim_ref[0]
        dt = jnp.exp(ldt_ref[0])
        mag = jnp.exp(are * dt)
        ang = aim * dt
        abr = mag * jnp.cos(ang)
        abi = mag * jnp.sin(ang)
        den = are * are + aim * aim
        nr = abr - 1.0
        f_re = (nr * are + abi * aim) / den
        f_im = (abi * are - nr * aim) / den

        pr = jnp.ones_like(abr)
        pi_ = jnp.zeros_like(abr)
        for k in range(L + 1):
            pr_ref[k:k + 1, :] = pr
            pi_ref[k:k + 1, :] = pi_
            pr, pi_ = pr * abr - pi_ * abi, pr * abi + pi_ * abr

        btr = btr_ref[0]
        bti = bti_ref[0]
        bbr = f_re * btr - f_im * bti
        bbi = f_re * bti + f_im * btr
        cr = cr_ref[0]
        ci = ci_ref[0]
        bb_hi, bb_lo = _split_bf16(jnp.concatenate([bbr, bbi], axis=1))
        tw_ref[(L - 1) * W:L * W, 0:W] = jnp.zeros((W, W), BF16)
        for k in range(L + 1):
            ar = pr_ref[k:k + 1, :]
            ai = pi_ref[k:k + 1, :]
            gk = jnp.concatenate([cr * ar - ci * ai, -(cr * ai + ci * ar)], axis=1)
            if k >= 1:
                et_ref[(k - 1) * W:k * W, :] = gk.astype(BF16)
            if k < L:
                wst_ref[(L - 1 - k) * W:(L - k) * W, :] = jnp.concatenate(
                    [bbr * ar - bbi * ai, bbr * ai + bbi * ar], axis=1).astype(BF16)
                g_hi, g_lo = _split_bf16(gk)
                bd = (_dot_nt(bb_hi, g_hi) + _dot_nt(bb_hi, g_lo) + _dot_nt(bb_lo, g_hi)).astype(BF16)
                for m in range(L):
                    for h in range(2):
                        if 2 * (L // 2 - 1 - m // 2) + h - m % 2 == k:
                            tw_ref[m * W:(m + 1) * W, h * W:(h + 1) * W] = bd
        cr_state[...] = s0r_ref[0]
        ci_state[...] = s0i_ref[0]

    for b in range(batch):
        for s in range(L):
            lhs_ref[b * nch_t:(b + 1) * nch_t, s * W:(s + 1) * W] = x_ref[b, pl.ds(s, nch_t, stride=L), :]

    nlb = 2 * PS // W
    for j in range(0, nlb, 2):
        v = _dot(lhs_ref[...].astype(BF16), wst_ref[:, j * W:(j + 2) * W])
        v_ref[j] = v[:, 0:W]
        v_ref[j + 1] = v[:, W:2 * W]

    alr = jnp.broadcast_to(pr_ref[L:L + 1, :], (batch, PS))
    ali = jnp.broadcast_to(pi_ref[L:L + 1, :], (batch, PS))
    sr = cr_state[...]
    si = ci_state[...]
    for n in range(nch_t):
        rows = pl.ds(n, batch, stride=nch_t)
        state = jnp.concatenate([sr, si], axis=1)
        for j in range(nlb):
            s_ref[j, rows, :] = state[:, j * W:(j + 1) * W]
        vn = jnp.concatenate([v_ref[j, rows, :] for j in range(nlb)], axis=1)
        sr, si = sr * alr - si * ali + vn[:, 0:PS], sr * ali + si * alr + vn[:, PS:2 * PS]
        if (final_chunk - 1) % nch_t == n:
            @pl.when(nt == (final_chunk - 1) // nch_t)
            def _(sr=sr, si=si):
                sfr_ref[0] = sr
                sfi_ref[0] = si
    cr_state[...] = sr
    ci_state[...] = si

    sb = jnp.concatenate([s_ref[j] for j in range(nlb)], axis=1).astype(BF16)
    for t2 in range(L // 2):
        kk = (t2 + 1) * 2 * W
        y2 = (_dot(lhs_ref[:, 0:kk].astype(BF16), tw_ref[(L // 2 - 1 - t2) * 2 * W:, :])
              + _dot_nt(sb, et_ref[t2 * 2 * W:(t2 + 1) * 2 * W, :]))
        for tl in range(2):
            for b in range(batch):
                y_ref[b, pl.ds(2 * t2 + tl, nch_t, stride=L), :] = y2[b * nch_t:(b + 1) * nch_t, tl * W:(tl + 1) * W]


def _block_lanes(p):
    g, pp = p.shape
    return p.reshape(g // SSM_BLOCK_GROUPS, 1, SSM_BLOCK_GROUPS * pp)


def _block_embed(t):
    g, c, pp = t.shape
    nb = g // SSM_BLOCK_GROUPS
    eye = jnp.eye(SSM_BLOCK_GROUPS, dtype=t.dtype)
    t = t.reshape(nb, SSM_BLOCK_GROUPS, c, 1, pp) * eye[None, :, None, :, None]
    return t.reshape(nb, SSM_BLOCK_GROUPS * c, SSM_BLOCK_GROUPS * pp)


def _ssm_params(a_re, a_im, log_dt, b_re, b_im, c_re, c_im):
    p = a_re.shape[1]
    return (_block_lanes(a_re), _block_lanes(a_im),
            _block_lanes(jnp.broadcast_to(log_dt[:, None], (log_dt.shape[0], p))),
            _block_embed(jnp.swapaxes(b_re, 1, 2)), _block_embed(jnp.swapaxes(b_im, 1, 2)),
            _block_embed(c_re), _block_embed(c_im))


def _ssm_scan(u, params, s0_re, s0_im, final_chunk):
    b, t, d = u.shape
    g = d // GROUP_CH
    L = SSM_CHUNK
    W = SSM_BLOCK_GROUPS * GROUP_CH
    PS = SSM_BLOCK_GROUPS * SSM_STATE
    nb = g // SSM_BLOCK_GROUPS
    nch = t // L
    nch_t = max(c for c in range(8, SSM_TILE_CHUNKS + 1, 8) if nch % c == 0)
    rows = b * nch_t
    s0r = s0_re.reshape(b, nb, PS).transpose(1, 0, 2)
    s0i = s0_im.reshape(b, nb, PS).transpose(1, 0, 2)
    vec = pl.BlockSpec((1, 1, PS), lambda i, j: (i, 0, 0))
    emb = pl.BlockSpec((1, W, PS), lambda i, j: (i, 0, 0))
    st = pl.BlockSpec((1, b, PS), lambda i, j: (i, 0, 0))
    xy = pl.BlockSpec((b, nch_t * L, W), lambda i, j: (0, j, i))
    y, sfr, sfi = pl.pallas_call(
        functools.partial(_ssm_kernel, nch_t=nch_t, batch=b, final_chunk=final_chunk),
        grid=(nb, nch // nch_t),
        in_specs=[xy, vec, vec, vec, emb, emb, emb, emb, st, st],
        out_specs=[xy, st, st],
        out_shape=[jax.ShapeDtypeStruct((b, t, d), F32),
                   jax.ShapeDtypeStruct((nb, b, PS), F32),
                   jax.ShapeDtypeStruct((nb, b, PS), F32)],
        scratch_shapes=[pltpu.VMEM((L + 8, PS), F32),
                        pltpu.VMEM((L + 8, PS), F32),
                        pltpu.VMEM((L * W, 2 * W), BF16),
                        pltpu.VMEM((L * W, 2 * PS), BF16),
                        pltpu.VMEM((L * W, 2 * PS), BF16),
                        pltpu.VMEM((rows, L * W), F32),
                        pltpu.VMEM((2 * PS // W, rows, W), F32),
                        pltpu.VMEM((2 * PS // W, rows, W), F32),
                        pltpu.VMEM((b, PS), F32),
                        pltpu.VMEM((b, PS), F32)],
        compiler_params=_cparams("parallel", "arbitrary"),
        name="ssm_scan",
    )(u, *params, s0r, s0i)
    sf_re = sfr.transpose(1, 0, 2).reshape(b, g, SSM_STATE)
    sf_im = sfi.transpose(1, 0, 2).reshape(b, g, SSM_STATE)
    return y, sf_re, sf_im


def _tri_ones(tk):
    j = jnp.arange(tk)[:, None]
    s = jnp.arange(tk)[None, :]
    return jnp.concatenate([(j > s).astype(BF16), jnp.ones((tk, HEAD_DIM), BF16)], axis=1)


def _sb_block(q, k, v, c, tri, mask):
    tq = q.shape[0]
    tk = k.shape[0]
    z = _dot_nt(q, k) * (HEAD_DIM ** -0.5)
    t = jnp.log1p(jnp.exp(-jnp.abs(z)))
    log_keep = -jnp.maximum(z, 0.0) - t
    log_beta = jnp.minimum(z, 0.0) - t
    if mask is not None:
        log_keep = jnp.where(mask, log_keep, 0.0)
    hi, lo = _split_bf16(log_keep)
    s2 = _dot(jnp.concatenate([hi, lo], axis=0), tri)
    s = s2[:tq] + s2[tq:]
    after = s[:, :tk]
    row_sum = s[:, tk:]
    log_w = log_beta + after
    if c is not None:
        log_w = log_w + (c if tk == HEAD_DIM else jnp.concatenate([c] * (tk // HEAD_DIM), axis=1))
    w = jnp.exp(log_w)
    if mask is not None:
        w = jnp.where(mask, w, 0.0)
    return _dot(w.astype(BF16), v), row_sum


def _attn_prompt_kernel(q_ref, k_ref, v_ref, triw_ref, tri_ref, o_ref, kb_ref, vb_ref, *, pad, unroll):
    t_len = q_ref.shape[1]
    tq = tk = ATT_TK
    wlen = ATT_WINDOW * tk
    front = wlen - tq
    zeros = jnp.zeros((front, HEAD_DIM), BF16)
    kb_ref[0:front, :] = zeros
    vb_ref[0:front, :] = zeros
    kb_ref[front:front + t_len, :] = k_ref[0].astype(BF16)
    vb_ref[front:front + t_len, :] = v_ref[0].astype(BF16)
    triw = triw_ref[...]

    def window(qi):
        q0 = pl.multiple_of(qi * tq, tq)
        q = q_ref[0, pl.ds(q0, tq), :].astype(BF16)
        qpos = q0 + lax.broadcasted_iota(jnp.int32, (tq, wlen), 0)
        kpos = q0 - front + lax.broadcasted_iota(jnp.int32, (tq, wlen), 1)
        mask = jnp.logical_and(kpos < qpos, kpos >= pad)
        pv, c = _sb_block(q, kb_ref[pl.ds(q0, wlen), :], vb_ref[pl.ds(q0, wlen), :], None, triw, mask)
        return q0, q, pv, c

    def tail(qi, q, acc, c):
        tri = tri_ref[...]

        def cond(state):
            kj, _, _, cmax = state
            return jnp.logical_and(kj >= 0, cmax > SKIP_LOG)

        def body(state):
            kj, acc, c, _ = state
            k0 = pl.multiple_of(kj * tk, tk)
            kpos = k0 + lax.broadcasted_iota(jnp.int32, (tq, tk), 1)
            pv, rs = _sb_block(q, kb_ref[pl.ds(k0 + front, tk), :], vb_ref[pl.ds(k0 + front, tk), :],
                               c, tri, kpos >= pad)
            c = c + rs
            return kj - 1, acc + pv, c, jnp.max(c)

        return lax.while_loop(cond, body, (qi - ATT_WINDOW, acc, c, jnp.max(c)))[1]

    def q_step(i, carry):
        blocks = [window(i * unroll + u) for u in range(unroll)]
        cmax = functools.reduce(jnp.maximum, [jnp.max(blk[3]) for blk in blocks])
        for u, (q0, q, pv, c) in enumerate(blocks):
            acc = lax.cond(cmax > SKIP_LOG, functools.partial(tail, i * unroll + u), lambda q, pv, c: pv, q, pv, c)
            o_ref[0, pl.ds(q0, tq), :] = acc.astype(o_ref.dtype)
        return carry

    lax.fori_loop(0, t_len // (tq * unroll), q_step, 0)


def _attn_prompt(qkv, pad):
    b, t, d3 = qkv.shape
    d = d3 // 3
    nh = d // HEAD_DIM
    unroll = _pick(t // ATT_TK, (3, 2, 1))
    wlen = ATT_WINDOW * ATT_TK
    col = lambda off: pl.BlockSpec((1, t, HEAD_DIM), lambda bi, hi: (bi, 0, off + hi))
    const = lambda tk: pl.BlockSpec((tk, tk + HEAD_DIM), lambda bi, hi: (0, 0))
    return pl.pallas_call(
        functools.partial(_attn_prompt_kernel, pad=pad, unroll=unroll),
        grid=(b, nh),
        in_specs=[col(0), col(nh), col(2 * nh), const(wlen), const(ATT_TK)],
        out_specs=pl.BlockSpec((1, t, HEAD_DIM), lambda bi, hi: (bi, 0, hi)),
        out_shape=jax.ShapeDtypeStruct((b, t, d), BF16),
        scratch_shapes=[pltpu.VMEM((t + wlen - ATT_TK, HEAD_DIM), BF16),
                        pltpu.VMEM((t + wlen - ATT_TK, HEAD_DIM), BF16)],
        compiler_params=_cparams("parallel", "parallel"),
        name="attn_prompt",
    )(qkv, qkv, qkv, _tri_ones(wlen), _tri_ones(ATT_TK))


def _attn_sample_kernel(q_ref, kn_ref, vn_ref, ck_ref, cv_ref, km_ref, vm_ref, tri_ref, o_ref, *, n_meta):
    tq = q_ref.shape[1]
    tk = ATT_TK
    past = ck_ref.shape[1]
    tri = tri_ref[...]
    q = q_ref[0].astype(BF16)
    qidx = lax.broadcasted_iota(jnp.int32, (tq, tk), 0)
    kidx = lax.broadcasted_iota(jnp.int32, (tq, tk), 1)

    acc, c = _sb_block(q, kn_ref[0].astype(BF16), vn_ref[0].astype(BF16), None, tri, kidx < qidx)

    def cond(state):
        kj, _, _, cmax = state
        return jnp.logical_and(kj >= 0, cmax > SKIP_LOG)

    def body(state):
        kj, acc, c, _ = state
        k0 = pl.multiple_of(kj * tk, tk)
        pv, rs = _sb_block(q, ck_ref[0, pl.ds(k0, tk), :].astype(BF16),
                           cv_ref[0, pl.ds(k0, tk), :].astype(BF16), c, tri, None)
        c = c + rs
        return kj - 1, acc + pv, c, jnp.max(c)

    _, acc, c, cmax = lax.while_loop(cond, body, (past // tk - 1, acc, c, jnp.max(c)))

    def meta_rows(acc, c):
        pv, _ = _sb_block(q, km_ref[...].astype(BF16), vm_ref[...].astype(BF16), c, tri, kidx < n_meta)
        return acc + pv

    acc = lax.cond(cmax > SKIP_LOG, meta_rows, lambda acc, c: acc, acc, c)
    o_ref[0] = acc.astype(o_ref.dtype)


def _attn_sample(qkv, cache_k, cache_v, meta_k, meta_v):
    b, s, d3 = qkv.shape
    d = d3 // 3
    nh = d // HEAD_DIM
    past = cache_k.shape[1]
    n_meta = meta_k.shape[0]
    assert s <= ATT_TK and n_meta <= ATT_TK and past % ATT_TK == 0
    k_new = jnp.pad(qkv[:, :, d:2 * d], ((0, 0), (0, ATT_TK - s), (0, 0)))
    v_new = jnp.pad(qkv[:, :, 2 * d:], ((0, 0), (0, ATT_TK - s), (0, 0)))
    meta_k = jnp.pad(meta_k, ((0, ATT_TK - n_meta), (0, 0)))
    meta_v = jnp.pad(meta_v, ((0, ATT_TK - n_meta), (0, 0)))
    head = lambda rows: pl.BlockSpec((1, rows, HEAD_DIM), lambda bi, hi: (bi, 0, hi))
    meta = pl.BlockSpec((ATT_TK, HEAD_DIM), lambda bi, hi: (0, hi))
    return pl.pallas_call(
        functools.partial(_attn_sample_kernel, n_meta=n_meta),
        grid=(b, nh),
        in_specs=[head(s), head(ATT_TK), head(ATT_TK), head(past), head(past), meta, meta,
                  pl.BlockSpec((ATT_TK, ATT_TK + HEAD_DIM), lambda bi, hi: (0, 0))],
        out_specs=head(s),
        out_shape=jax.ShapeDtypeStruct((b, s, d), BF16),
        compiler_params=_cparams("parallel", "parallel"),
        name="attn_sample",
    )(qkv, k_new, v_new, cache_k, cache_v, meta_k, meta_v, _tri_ones(ATT_TK))


def kernel(x_prompt, x_sample, state_ssm_re, state_ssm_im, cache_k, cache_v, meta_tokens, norm_mix, norm_mlp, ssm_a_re, ssm_a_im, ssm_log_dt, ssm_b_re, ssm_b_im, ssm_c_re, ssm_c_im, ssm_d, ssm_w_glu, sb_w_qkv, sb_q_norm, sb_k_norm, sb_w_o, mlp_w_up, mlp_w_down):
    b, seq, d = x_prompt.shape
    db, ds, _ = x_sample.shape
    depth = norm_mix.shape[0]
    n_meta = meta_tokens.shape[0]
    nh = d // HEAD_DIM
    past = cache_k.shape[2]
    assert ds % SSM_CHUNK == 0, "the running streams advance by whole S5 chunks"
    ds_tile = -(-ds // (8 * SSM_CHUNK)) * 8 * SSM_CHUNK
    t_real = n_meta + seq
    t_len = -(-t_real // ROW_ALIGN) * ROW_ALIGN
    pad = t_len - t_real

    meta = jnp.broadcast_to(meta_tokens[None].astype(F32), (b, n_meta, d))
    h_p = jnp.concatenate([jnp.zeros((b, pad, d), F32), meta, x_prompt], axis=1).reshape(b * t_len, d)
    h_s = x_sample.reshape(db * ds, d)
    zero_state = jnp.zeros((b, d // GROUP_CH, SSM_STATE), F32)

    outs = {name: [] for name in ("re_p", "im_p", "k_p", "v_p", "re_s", "im_s", "k_s", "v_s")}
    i_ssm = i_sb = 0
    for layer in range(depth):
        if layer % 2 == 0:
            params = _ssm_params(ssm_a_re[i_ssm], ssm_a_im[i_ssm], ssm_log_dt[i_ssm], ssm_b_re[i_ssm],
                                 ssm_b_im[i_ssm], ssm_c_re[i_ssm], ssm_c_im[i_ssm])
            w_glu = ssm_w_glu[i_ssm].astype(BF16)
            u_p = _rmsnorm(h_p, norm_mix[layer], F32)
            y_p, re_p, im_p = _ssm_scan(u_p.reshape(b, t_len, d), params, zero_state, zero_state, t_len // SSM_CHUNK)
            h_p = _glu(y_p.reshape(b * t_len, d), u_p, h_p, ssm_d[i_ssm], w_glu)
            u_s = _rmsnorm(h_s, norm_mix[layer], F32)
            u_s_tile = jnp.pad(u_s.reshape(db, ds, d), ((0, 0), (0, ds_tile - ds), (0, 0)))
            y_s, re_s, im_s = _ssm_scan(u_s_tile, params, state_ssm_re[i_ssm], state_ssm_im[i_ssm], ds // SSM_CHUNK)
            h_s = _glu(y_s[:, :ds].reshape(db * ds, d), u_s, h_s, ssm_d[i_ssm], w_glu)
            outs["re_p"].append(re_p)
            outs["im_p"].append(im_p)
            outs["re_s"].append(re_s)
            outs["im_s"].append(im_s)
            i_ssm += 1
        else:
            w_qkv = sb_w_qkv[i_sb].astype(BF16)
            w_o = sb_w_o[i_sb].astype(BF16)
            gains = jnp.stack([sb_q_norm[i_sb], sb_k_norm[i_sb], jnp.ones_like(sb_q_norm[i_sb])]).reshape(3, 1, HEAD_DIM)
            qkv_p = _norm_matmul(h_p, norm_mix[layer], w_qkv, "qkv", F32, gains).reshape(b, t_len, 3 * d)
            o_p = _attn_prompt(qkv_p, pad)
            h_p = _matmul_resid(o_p.reshape(b * t_len, d), w_o, h_p)
            qkv_s = _norm_matmul(h_s, norm_mix[layer], w_qkv, "qkv", F32, gains).reshape(db, ds, 3 * d)
            o_s = _attn_sample(qkv_s, cache_k[i_sb].reshape(db, past, d), cache_v[i_sb].reshape(db, past, d),
                               qkv_p[0, pad:pad + n_meta, d:2 * d], qkv_p[0, pad:pad + n_meta, 2 * d:])
            h_s = _matmul_resid(o_s.reshape(db * ds, d), w_o, h_s)
            outs["k_p"].append(qkv_p[:, pad:, d:2 * d].reshape(b, t_real, nh, HEAD_DIM))
            outs["v_p"].append(qkv_p[:, pad:, 2 * d:].reshape(b, t_real, nh, HEAD_DIM))
            outs["k_s"].append(qkv_s[:, :, d:2 * d].reshape(db, ds, nh, HEAD_DIM))
            outs["v_s"].append(qkv_s[:, :, 2 * d:].reshape(db, ds, nh, HEAD_DIM))
            i_sb += 1
        w_up = mlp_w_up[layer].astype(BF16)
        w_down = mlp_w_down[layer].astype(BF16)
        h_p = _matmul_resid(_norm_matmul(h_p, norm_mlp[layer], w_up, "relu2", BF16), w_down, h_p)
        h_s = _matmul_resid(_norm_matmul(h_s, norm_mlp[layer], w_up, "relu2", BF16), w_down, h_s)

    y_prompt = h_p.reshape(b, t_len, d)[:, pad + n_meta:]
    y_sample = h_s.reshape(db, ds, d)
    st = {k: jnp.stack(v) for k, v in outs.items()}
    return (y_prompt, y_sample, st["re_p"], st["im_p"], st["k_p"], st["v_p"],
            st["re_s"], st["im_s"], st["k_s"], st["v_s"])
```

```python
import functools
import math

import jax
import jax.numpy as jnp
from jax import lax
from jax.experimental import pallas as pl
from jax.experimental.pallas import tpu as pltpu

F32 = jnp.float32
BF16 = jnp.bfloat16

NORM_EPS = 1e-6
HEAD_DIM = 128
GROUP_CH = 16
SSM_STATE = 64
SSM_CHUNK = 16
SSM_BLOCK_GROUPS = 8
SSM_TILE_CHUNKS = 88
ATT_TK = 128
ATT_WINDOW = 3
SKIP_LOG = -100.0
ROW_ALIGN = 128
VMEM_LIMIT_BYTES = 56 * 1024 * 1024


def _cparams(*sem):
    return pltpu.CompilerParams(dimension_semantics=sem, vmem_limit_bytes=VMEM_LIMIT_BYTES)


def _pick(n, candidates):
    for c in candidates:
        if n % c == 0:
            return c
    raise ValueError(f"no tile in {candidates} divides {n}")


def _dot(a, b):
    return jnp.dot(a, b, preferred_element_type=F32)


def _dot_nt(a, b):
    return lax.dot_general(a, b, (((1,), (1,)), ((), ())), preferred_element_type=F32)


def _split_bf16(x):
    hi = x.astype(BF16)
    lo = (x - hi.astype(F32)).astype(BF16)
    return hi, lo


def _rms(x, gain):
    ms = jnp.mean(x * x, axis=-1, keepdims=True)
    return x * lax.rsqrt(ms + NORM_EPS) * gain


def _rmsnorm_kernel(x_ref, g_ref, o_ref):
    o_ref[...] = _rms(x_ref[...], g_ref[...]).astype(o_ref.dtype)


def _rmsnorm(x, gain, out_dtype):
    m, d = x.shape
    tm = _pick(m, (512, 256, 128))
    return pl.pallas_call(
        _rmsnorm_kernel,
        grid=(m // tm,),
        in_specs=[pl.BlockSpec((tm, d), lambda i: (i, 0)),
                  pl.BlockSpec((1, d), lambda i: (0, 0))],
        out_specs=pl.BlockSpec((tm, d), lambda i: (i, 0)),
        out_shape=jax.ShapeDtypeStruct((m, d), out_dtype),
        compiler_params=_cparams("parallel"),
        name="rmsnorm",
    )(x, gain.reshape(1, d))


def _norm_matmul_kernel(x_ref, g_ref, w_ref, *rest, epilogue, norm_tiles):
    if epilogue == "qkv":
        hg_ref, o_ref, xn_ref = rest
    else:
        o_ref, xn_ref = rest
    j = pl.program_id(1)

    @pl.when(j == 0)
    def _():
        xn_ref[...] = _rms(x_ref[...], g_ref[...]).astype(BF16)

    acc = _dot(xn_ref[...], w_ref[...])
    if epilogue == "relu2":
        o_ref[...] = jnp.square(jnp.maximum(acc, 0.0)).astype(o_ref.dtype)
    else:
        @pl.when(j < norm_tiles)
        def _():
            for hh in range(acc.shape[1] // HEAD_DIM):
                seg = acc[:, hh * HEAD_DIM:(hh + 1) * HEAD_DIM]
                o_ref[:, hh * HEAD_DIM:(hh + 1) * HEAD_DIM] = _rms(seg, hg_ref[0])

        @pl.when(j >= norm_tiles)
        def _():
            o_ref[...] = acc


def _norm_matmul(x, gain, w, epilogue, out_dtype, head_gains=None):
    m, d = x.shape
    n = w.shape[1]
    tm = _pick(m, (1024, 512, 256, 128))
    tn = _pick(d, (1024, 512, 256, 128)) if epilogue == "qkv" else _pick(n, (1024, 512, 256, 128))
    in_specs = [pl.BlockSpec((tm, d), lambda i, j: (i, 0)),
                pl.BlockSpec((1, d), lambda i, j: (0, 0)),
                pl.BlockSpec((d, tn), lambda i, j: (0, j))]
    args = [x, gain.reshape(1, d), w]
    norm_tiles = 0
    if epilogue == "qkv":
        tiles_per_part = d // tn
        norm_tiles = 2 * tiles_per_part
        in_specs.append(pl.BlockSpec((1, 1, HEAD_DIM), lambda i, j: (j // tiles_per_part, 0, 0)))
        args.append(head_gains)
    return pl.pallas_call(
        functools.partial(_norm_matmul_kernel, epilogue=epilogue, norm_tiles=norm_tiles),
        grid=(m // tm, n // tn),
        in_specs=in_specs,
        out_specs=pl.BlockSpec((tm, tn), lambda i, j: (i, j)),
        out_shape=jax.ShapeDtypeStruct((m, n), out_dtype),
        scratch_shapes=[pltpu.VMEM((tm, d), BF16)],
        compiler_params=_cparams("parallel", "arbitrary"),
        name="norm_matmul_" + epilogue,
    )(*args)


def _qkv_prompt_kernel(x_ref, g_ref, w_ref, hg_ref, k_in, v_in, o_ref, ko_hbm, vo_hbm, xn_ref, kbuf, vbuf, sem,
                       *, tiles_per_part, layer, row_tiles, t_real):
    del k_in, v_in
    bi = pl.program_id(0)
    i = pl.program_id(1)
    j = pl.program_id(2)

    @pl.when(j == 0)
    def _():
        xn_ref[...] = _rms(x_ref[0], g_ref[...]).astype(BF16)

    acc = _dot(xn_ref[...], w_ref[...])
    tm, tn = acc.shape
    heads_per_tile = tn // HEAD_DIM
    nheads = tiles_per_part * heads_per_tile
    rows_full = tm * nheads
    rows_last = (t_real - (row_tiles - 1) * tm) * nheads

    def for_tile_rows(fn):
        if rows_last == rows_full:
            fn(rows_full)
        else:
            pl.when(i < row_tiles - 1)(lambda: fn(rows_full))
            pl.when(i == row_tiles - 1)(lambda: fn(rows_last))

    def copy(buf, hbm, slot, rows):
        dst = hbm.at[layer, bi, pl.ds(pl.multiple_of(i * rows_full, rows_full), rows)]
        return pltpu.make_async_copy(buf.at[pl.ds(0, rows)], dst, sem.at[slot])

    for jj in range(3 * tiles_per_part):
        part = jj // tiles_per_part
        head0 = (jj % tiles_per_part) * heads_per_tile

        @pl.when(j == jj)
        def _(jj=jj, part=part, head0=head0):
            for hh in range(heads_per_tile):
                seg = acc[:, hh * HEAD_DIM:(hh + 1) * HEAD_DIM]
                if part < 2:
                    seg = _rms(seg, hg_ref[0])
                o_ref[0, :, hh * HEAD_DIM:(hh + 1) * HEAD_DIM] = seg.astype(o_ref.dtype)
                if part == 1:
                    kbuf[pl.ds(head0 + hh, tm, stride=nheads), :] = seg
                if part == 2:
                    vbuf[pl.ds(head0 + hh, tm, stride=nheads), :] = seg
            if jj == 2 * tiles_per_part - 1:
                for_tile_rows(lambda rows: copy(kbuf, ko_hbm, 0, rows).start())
            if jj == 3 * tiles_per_part - 1:
                def finish(rows):
                    copy(vbuf, vo_hbm, 1, rows).start()
                    copy(kbuf, ko_hbm, 0, rows).wait()
                    copy(vbuf, vo_hbm, 1, rows).wait()
                for_tile_rows(finish)


def _qkv_prompt(h, gain, w, head_gains, t_real, layer, kv_prev):
    b, t, d = h.shape
    nheads = d // HEAD_DIM
    tm = _pick(t, (384, 256, 128))
    tn = _pick(d, (1024, 512, 256, 128))
    tiles_per_part = d // tn
    row_tiles = t // tm
    assert (row_tiles - 1) * tm < t_real <= t
    in_specs = [pl.BlockSpec((1, tm, d), lambda bi, i, j: (bi, i, 0)),
                pl.BlockSpec((1, d), lambda bi, i, j: (0, 0)),
                pl.BlockSpec((d, tn), lambda bi, i, j: (0, j)),
                pl.BlockSpec((1, 1, HEAD_DIM), lambda bi, i, j: (j // tiles_per_part, 0, 0)),
                pl.BlockSpec(memory_space=pl.ANY), pl.BlockSpec(memory_space=pl.ANY)]
    kv_shape = jax.ShapeDtypeStruct(kv_prev[0].shape, F32)
    assert kv_prev[0].shape[1:] == (b, t_real * nheads, HEAD_DIM)
    qkv, k_out, v_out = pl.pallas_call(
        functools.partial(_qkv_prompt_kernel, tiles_per_part=tiles_per_part, layer=layer,
                          row_tiles=row_tiles, t_real=t_real),
        grid=(b, row_tiles, 3 * tiles_per_part),
        in_specs=in_specs,
        out_specs=[pl.BlockSpec((1, tm, tn), lambda bi, i, j: (bi, i, j)),
                   pl.BlockSpec(memory_space=pl.ANY), pl.BlockSpec(memory_space=pl.ANY)],
        out_shape=[jax.ShapeDtypeStruct((b, t, 3 * d), BF16), kv_shape, kv_shape],
        scratch_shapes=[pltpu.VMEM((tm, d), BF16),
                        pltpu.VMEM((tm * nheads, HEAD_DIM), F32),
                        pltpu.VMEM((tm * nheads, HEAD_DIM), F32),
                        pltpu.SemaphoreType.DMA((2,))],
        input_output_aliases={4: 1, 5: 2},
        compiler_params=_cparams("parallel", "arbitrary", "arbitrary"),
        name="qkv_prompt",
    )(h, gain.reshape(1, d), w, head_gains, *kv_prev)
    return qkv, (k_out, v_out)


def _matmul_resid_kernel(x_ref, w_ref, r_ref, o_ref):
    @pl.when(pl.program_id(2) == 0)
    def _():
        o_ref[...] = r_ref[...]

    o_ref[...] += _dot(x_ref[...], w_ref[...])


def _matmul_resid(x, w, resid):
    m, k = x.shape
    n = w.shape[1]
    tm = _pick(m, (1024, 512, 256, 128))
    tn = _pick(n, (1024, 512, 256, 128))
    tk = _pick(k, (2048, 1024, 512, 256, 128))
    return pl.pallas_call(
        _matmul_resid_kernel,
        grid=(m // tm, n // tn, k // tk),
        in_specs=[pl.BlockSpec((tm, tk), lambda i, j, kk: (i, kk)),
                  pl.BlockSpec((tk, tn), lambda i, j, kk: (kk, j)),
                  pl.BlockSpec((tm, tn), lambda i, j, kk: (i, j))],
        out_specs=pl.BlockSpec((tm, tn), lambda i, j, kk: (i, j)),
        out_shape=jax.ShapeDtypeStruct((m, n), F32),
        compiler_params=_cparams("parallel", "parallel", "arbitrary"),
        name="matmul_resid",
    )(x, w, resid)


def _glu_kernel(y_ref, u_ref, h_ref, d_ref, w_ref, o_ref):
    g = jax.nn.gelu(y_ref[...] + d_ref[...] * u_ref[...])
    gate = jax.nn.sigmoid(_dot(g.astype(BF16), w_ref[...]))
    o_ref[...] = h_ref[...] + g * gate


def _glu(y, u, h, d_skip, w):
    m, d = y.shape
    tm = _pick(m, (256, 128))
    row = pl.BlockSpec((tm, d), lambda i: (i, 0))
    return pl.pallas_call(
        _glu_kernel,
        grid=(m // tm,),
        in_specs=[row, row, row,
                  pl.BlockSpec((1, d), lambda i: (0, 0)),
                  pl.BlockSpec((d, d), lambda i: (0, 0))],
        out_specs=row,
        out_shape=jax.ShapeDtypeStruct((m, d), F32),
        compiler_params=_cparams("parallel"),
        name="ssm_glu",
    )(y, u, h, d_skip.reshape(1, d), w)


def _ssm_kernel(x_ref, are_ref, aim_ref, ldt_ref, btr_ref, bti_ref, cr_ref, ci_ref, s0r_ref, s0i_ref,
                y_ref, sfr_ref, sfi_ref,
                pr_ref, pi_ref, tw_ref, wst_ref, et_ref, lhs_ref, v_ref, s_ref, cr_state, ci_state,
                *, nch_t, batch, final_chunk):
    L = SSM_CHUNK
    W = SSM_BLOCK_GROUPS * GROUP_CH
    PS = SSM_BLOCK_GROUPS * SSM_STATE
    nt = pl.program_id(1)

    @pl.when(nt == 0)
    def _():
        are = are_ref[0]
        aim = aim_ref[0]
        dt = jnp.exp(ldt_ref[0])
        mag = jnp.exp(are * dt)
        ang = aim * dt
        abr = mag * jnp.cos(ang)
        abi = mag * jnp.sin(ang)
        den = are * are + aim * aim
        nr = abr - 1.0
        f_re = (nr * are + abi * aim) / den
        f_im = (abi * are - nr * aim) / den

        pr = jnp.ones_like(abr)
        pi_ = jnp.zeros_like(abr)
        for k in range(L + 1):
            pr_ref[k:k + 1, :] = pr
            pi_ref[k:k + 1, :] = pi_
            pr, pi_ = pr * abr - pi_ * abi, pr * abi + pi_ * abr

        btr = btr_ref[0]
        bti = bti_ref[0]
        bbr = f_re * btr - f_im * bti
        bbi = f_re * bti + f_im * btr
        cr = cr_ref[0]
        ci = ci_ref[0]
        bb_hi, bb_lo = _split_bf16(jnp.concatenate([bbr, bbi], axis=1))
        tw_ref[(L - 1) * W:L * W, 0:W] = jnp.zeros((W, W), BF16)
        for k in range(L + 1):
            ar = pr_ref[k:k + 1, :]
            ai = pi_ref[k:k + 1, :]
            gk = jnp.concatenate([cr * ar - ci * ai, -(cr * ai + ci * ar)], axis=1)
            if k >= 1:
                et_ref[(k - 1) * W:k * W, :] = gk.astype(BF16)
            if k < L:
                wst_ref[(L - 1 - k) * W:(L - k) * W, :] = jnp.concatenate(
                    [bbr * ar - bbi * ai, bbr * ai + bbi * ar], axis=1).astype(BF16)
                g_hi, g_lo = _split_bf16(gk)
                bd = (_dot_nt(bb_hi, g_hi) + _dot_nt(bb_hi, g_lo) + _dot_nt(bb_lo, g_hi)).astype(BF16)
                for m in range(L):
                    for h in range(2):
                        if 2 * (L // 2 - 1 - m // 2) + h - m % 2 == k:
                            tw_ref[m * W:(m + 1) * W, h * W:(h + 1) * W] = bd
        cr_state[...] = s0r_ref[0]
        ci_state[...] = s0i_ref[0]

    for b in range(batch):
        for s in range(L):
            lhs_ref[b * nch_t:(b + 1) * nch_t, s * W:(s + 1) * W] = x_ref[b, pl.ds(s, nch_t, stride=L), :]

    nlb = 2 * PS // W
    for j in range(0, nlb, 2):
        v = _dot(lhs_ref[...].astype(BF16), wst_ref[:, j * W:(j + 2) * W])
        v_ref[j] = v[:, 0:W]
        v_ref[j + 1] = v[:, W:2 * W]

    alr = jnp.broadcast_to(pr_ref[L:L + 1, :], (batch, PS))
    ali = jnp.broadcast_to(pi_ref[L:L + 1, :], (batch, PS))
    sr = cr_state[...]
    si = ci_state[...]
    for n in range(nch_t):
        rows = pl.ds(n, batch, stride=nch_t)
        state = jnp.concatenate([sr, si], axis=1)
        for j in range(nlb):
            s_ref[j, rows, :] = state[:, j * W:(j + 1) * W]
        vn = jnp.concatenate([v_ref[j, rows, :] for j in range(nlb)], axis=1)
        sr, si = sr * alr - si * ali + vn[:, 0:PS], sr * ali + si * alr + vn[:, PS:2 * PS]
        if (final_chunk - 1) % nch_t == n:
            @pl.when(nt == (final_chunk - 1) // nch_t)
            def _(sr=sr, si=si):
                sfr_ref[0] = sr
                sfi_ref[0] = si
    cr_state[...] = sr
    ci_state[...] = si

    sb = jnp.concatenate([s_ref[j] for j in range(nlb)], axis=1).astype(BF16)
    for t2 in range(L // 2):
        kk = (t2 + 1) * 2 * W
        y2 = (_dot(lhs_ref[:, 0:kk].astype(BF16), tw_ref[(L // 2 - 1 - t2) * 2 * W:, :])
              + _dot_nt(sb, et_ref[t2 * 2 * W:(t2 + 1) * 2 * W, :]))
        for tl in range(2):
            for b in range(batch):
                y_ref[b, pl.ds(2 * t2 + tl, nch_t, stride=L), :] = y2[b * nch_t:(b + 1) * nch_t, tl * W:(tl + 1) * W]


def _block_lanes(p):
    g, pp = p.shape
    return p.reshape(g // SSM_BLOCK_GROUPS, 1, SSM_BLOCK_GROUPS * pp)


def _block_embed(t):
    g, c, pp = t.shape
    nb = g // SSM_BLOCK_GROUPS
    eye = jnp.eye(SSM_BLOCK_GROUPS, dtype=t.dtype)
    t = t.reshape(nb, SSM_BLOCK_GROUPS, c, 1, pp) * eye[None, :, None, :, None]
    return t.reshape(nb, SSM_BLOCK_GROUPS * c, SSM_BLOCK_GROUPS * pp)


def _ssm_params(a_re, a_im, log_dt, b_re, b_im, c_re, c_im):
    p = a_re.shape[1]
    return (_block_lanes(a_re), _block_lanes(a_im),
            _block_lanes(jnp.broadcast_to(log_dt[:, None], (log_dt.shape[0], p))),
            _block_embed(jnp.swapaxes(b_re, 1, 2)), _block_embed(jnp.swapaxes(b_im, 1, 2)),
            _block_embed(c_re), _block_embed(c_im))


def _ssm_scan(u, params, s0_re, s0_im, final_chunk):
    b, t, d = u.shape
    g = d // GROUP_CH
    L = SSM_CHUNK
    W = SSM_BLOCK_GROUPS * GROUP_CH
    PS = SSM_BLOCK_GROUPS * SSM_STATE
    nb = g // SSM_BLOCK_GROUPS
    nch = t // L
    nch_t = max(c for c in range(8, SSM_TILE_CHUNKS + 1, 8) if nch % c == 0)
    rows = b * nch_t
    s0r = s0_re.reshape(b, nb, PS).transpose(1, 0, 2)
    s0i = s0_im.reshape(b, nb, PS).transpose(1, 0, 2)
    vec = pl.BlockSpec((1, 1, PS), lambda i, j: (i, 0, 0))
    emb = pl.BlockSpec((1, W, PS), lambda i, j: (i, 0, 0))
    st = pl.BlockSpec((1, b, PS), lambda i, j: (i, 0, 0))
    xy = pl.BlockSpec((b, nch_t * L, W), lambda i, j: (0, j, i))
    y, sfr, sfi = pl.pallas_call(
        functools.partial(_ssm_kernel, nch_t=nch_t, batch=b, final_chunk=final_chunk),
        grid=(nb, nch // nch_t),
        in_specs=[xy, vec, vec, vec, emb, emb, emb, emb, st, st],
        out_specs=[xy, st, st],
        out_shape=[jax.ShapeDtypeStruct((b, t, d), F32),
                   jax.ShapeDtypeStruct((nb, b, PS), F32),
                   jax.ShapeDtypeStruct((nb, b, PS), F32)],
        scratch_shapes=[pltpu.VMEM((L + 8, PS), F32),
                        pltpu.VMEM((L + 8, PS), F32),
                        pltpu.VMEM((L * W, 2 * W), BF16),
                        pltpu.VMEM((L * W, 2 * PS), BF16),
                        pltpu.VMEM((L * W, 2 * PS), BF16),
                        pltpu.VMEM((rows, L * W), F32),
                        pltpu.VMEM((2 * PS // W, rows, W), F32),
                        pltpu.VMEM((2 * PS // W, rows, W), F32),
                        pltpu.VMEM((b, PS), F32),
                        pltpu.VMEM((b, PS), F32)],
        compiler_params=_cparams("parallel", "arbitrary"),
        name="ssm_scan",
    )(u, *params, s0r, s0i)
    sf_re = sfr.transpose(1, 0, 2).reshape(b, g, SSM_STATE)
    sf_im = sfi.transpose(1, 0, 2).reshape(b, g, SSM_STATE)
    return y, sf_re, sf_im


def _tri_ones(tk):
    j = jnp.arange(tk)[:, None]
    s = jnp.arange(tk)[None, :]
    return jnp.concatenate([(j > s).astype(BF16), jnp.ones((tk, HEAD_DIM), BF16)], axis=1)


def _sb_block(q, k, v, c, tri, mask):
    tq = q.shape[0]
    tk = k.shape[0]
    z = _dot_nt(q, k) * (HEAD_DIM ** -0.5)
    t = jnp.log(1.0 + jnp.exp(-jnp.abs(z)))
    log_keep = -jnp.maximum(z, 0.0) - t
    log_beta = jnp.minimum(z, 0.0) - t
    if mask is not None:
        log_keep = jnp.where(mask, log_keep, 0.0)
    hi, lo = _split_bf16(log_keep)
    s2 = _dot(jnp.concatenate([hi, lo], axis=0), tri)
    s = s2[:tq] + s2[tq:]
    after = s[:, :tk]
    row_sum = s[:, tk:]
    log_w = log_beta + after
    if c is not None:
        log_w = log_w + (c if tk == HEAD_DIM else jnp.concatenate([c] * (tk // HEAD_DIM), axis=1))
    w = jnp.exp(log_w)
    if mask is not None:
        w = jnp.where(mask, w, 0.0)
    return _dot(w.astype(BF16), v), row_sum


def _attn_prompt_kernel(q_ref, k_ref, v_ref, triw_ref, tri_ref, o_ref, kb_ref, vb_ref, *, unroll):
    t_len = q_ref.shape[1]
    tq = tk = ATT_TK
    wlen = ATT_WINDOW * tk
    front = wlen - tq
    zeros = jnp.zeros((front, HEAD_DIM), BF16)
    kb_ref[0:front, :] = zeros
    vb_ref[0:front, :] = zeros
    kb_ref[front:front + t_len, :] = k_ref[0]
    vb_ref[front:front + t_len, :] = v_ref[0]
    triw = triw_ref[...]

    def window(qi):
        q0 = pl.multiple_of(qi * tq, tq)
        q = q_ref[0, pl.ds(q0, tq), :]
        qpos = q0 + lax.broadcasted_iota(jnp.int32, (tq, wlen), 0)
        kpos = q0 - front + lax.broadcasted_iota(jnp.int32, (tq, wlen), 1)
        mask = jnp.logical_and(kpos < qpos, kpos >= 0)
        pv, c = _sb_block(q, kb_ref[pl.ds(q0, wlen), :], vb_ref[pl.ds(q0, wlen), :], None, triw, mask)
        return q0, q, pv, c

    def tail(qi, q, acc, c):
        tri = tri_ref[...]

        def cond(state):
            kj, _, _, cmax = state
            return jnp.logical_and(kj >= 0, cmax > SKIP_LOG)

        def body(state):
            kj, acc, c, _ = state
            k0 = pl.multiple_of(kj * tk, tk)
            pv, rs = _sb_block(q, kb_ref[pl.ds(k0 + front, tk), :], vb_ref[pl.ds(k0 + front, tk), :],
                               c, tri, None)
            c = c + rs
            return kj - 1, acc + pv, c, jnp.max(c)

        return lax.while_loop(cond, body, (qi - ATT_WINDOW, acc, c, jnp.max(c)))[1]

    def q_step(i, carry):
        blocks = [window(i * unroll + u) for u in range(unroll)]
        cmax = functools.reduce(jnp.maximum, [jnp.max(blk[3]) for blk in blocks])
        for u, (q0, q, pv, c) in enumerate(blocks):
            acc = lax.cond(cmax > SKIP_LOG, functools.partial(tail, i * unroll + u), lambda q, pv, c: pv, q, pv, c)
            o_ref[0, pl.ds(q0, tq), :] = acc.astype(o_ref.dtype)
        return carry

    lax.fori_loop(0, t_len // (tq * unroll), q_step, 0)


def _attn_prompt(qkv):
    b, t, d3 = qkv.shape
    d = d3 // 3
    nh = d // HEAD_DIM
    unroll = _pick(t // ATT_TK, (3, 2, 1))
    wlen = ATT_WINDOW * ATT_TK
    col = lambda off: pl.BlockSpec((1, t, HEAD_DIM), lambda bi, hi: (bi, 0, off + hi))
    const = lambda tk: pl.BlockSpec((tk, tk + HEAD_DIM), lambda bi, hi: (0, 0))
    return pl.pallas_call(
        functools.partial(_attn_prompt_kernel, unroll=unroll),
        grid=(b, nh),
        in_specs=[col(0), col(nh), col(2 * nh), const(wlen), const(ATT_TK)],
        out_specs=pl.BlockSpec((1, t, HEAD_DIM), lambda bi, hi: (bi, 0, hi)),
        out_shape=jax.ShapeDtypeStruct((b, t, d), BF16),
        scratch_shapes=[pltpu.VMEM((t + wlen - ATT_TK, HEAD_DIM), BF16),
                        pltpu.VMEM((t + wlen - ATT_TK, HEAD_DIM), BF16)],
        compiler_params=_cparams("parallel", "parallel"),
        name="attn_prompt",
    )(qkv, qkv, qkv, _tri_ones(wlen), _tri_ones(ATT_TK))


def _attn_sample_kernel(q_ref, kn_ref, vn_ref, ck_ref, cv_ref, km_ref, vm_ref, tri_ref, o_ref, *, n_meta):
    tq = q_ref.shape[1]
    tk = ATT_TK
    past = ck_ref.shape[1]
    tri = tri_ref[...]
    q = q_ref[0].astype(BF16)
    qidx = lax.broadcasted_iota(jnp.int32, (tq, tk), 0)
    kidx = lax.broadcasted_iota(jnp.int32, (tq, tk), 1)

    acc, c = _sb_block(q, kn_ref[0].astype(BF16), vn_ref[0].astype(BF16), None, tri, kidx < qidx)

    def cond(state):
        kj, _, _, cmax = state
        return jnp.logical_and(kj >= 0, cmax > SKIP_LOG)

    def body(state):
        kj, acc, c, _ = state
        k0 = pl.multiple_of(kj * tk, tk)
        pv, rs = _sb_block(q, ck_ref[0, pl.ds(k0, tk), :].astype(BF16),
                           cv_ref[0, pl.ds(k0, tk), :].astype(BF16), c, tri, None)
        c = c + rs
        return kj - 1, acc + pv, c, jnp.max(c)

    _, acc, c, cmax = lax.while_loop(cond, body, (past // tk - 1, acc, c, jnp.max(c)))

    def meta_rows(acc, c):
        pv, _ = _sb_block(q, km_ref[...].astype(BF16), vm_ref[...].astype(BF16), c, tri, kidx < n_meta)
        return acc + pv

    acc = lax.cond(cmax > SKIP_LOG, meta_rows, lambda acc, c: acc, acc, c)
    o_ref[0] = acc.astype(o_ref.dtype)


def _attn_sample(qkv, cache_k, cache_v, meta_k, meta_v):
    b, s, d3 = qkv.shape
    d = d3 // 3
    nh = d // HEAD_DIM
    past = cache_k.shape[1]
    n_meta = meta_k.shape[0]
    assert s <= ATT_TK and n_meta <= ATT_TK and past % ATT_TK == 0
    k_new = jnp.pad(qkv[:, :, d:2 * d], ((0, 0), (0, ATT_TK - s), (0, 0)))
    v_new = jnp.pad(qkv[:, :, 2 * d:], ((0, 0), (0, ATT_TK - s), (0, 0)))
    meta_k = jnp.pad(meta_k, ((0, ATT_TK - n_meta), (0, 0)))
    meta_v = jnp.pad(meta_v, ((0, ATT_TK - n_meta), (0, 0)))
    head = lambda rows: pl.BlockSpec((1, rows, HEAD_DIM), lambda bi, hi: (bi, 0, hi))
    meta = pl.BlockSpec((ATT_TK, HEAD_DIM), lambda bi, hi: (0, hi))
    return pl.pallas_call(
        functools.partial(_attn_sample_kernel, n_meta=n_meta),
        grid=(b, nh),
        in_specs=[head(s), head(ATT_TK), head(ATT_TK), head(past), head(past), meta, meta,
                  pl.BlockSpec((ATT_TK, ATT_TK + HEAD_DIM), lambda bi, hi: (0, 0))],
        out_specs=head(s),
        out_shape=jax.ShapeDtypeStruct((b, s, d), BF16),
        compiler_params=_cparams("parallel", "parallel"),
        name="attn_sample",
    )(qkv, k_new, v_new, cache_k, cache_v, meta_k, meta_v, _tri_ones(ATT_TK))


def kernel(x_prompt, x_sample, state_ssm_re, state_ssm_im, cache_k, cache_v, meta_tokens, norm_mix, norm_mlp, ssm_a_re, ssm_a_im, ssm_log_dt, ssm_b_re, ssm_b_im, ssm_c_re, ssm_c_im, ssm_d, ssm_w_glu, sb_w_qkv, sb_q_norm, sb_k_norm, sb_w_o, mlp_w_up, mlp_w_down):
    b, seq, d = x_prompt.shape
    db, ds, _ = x_sample.shape
    depth = norm_mix.shape[0]
    n_meta = meta_tokens.shape[0]
    nh = d // HEAD_DIM
    past = cache_k.shape[2]
    assert ds % SSM_CHUNK == 0, "the running streams advance by whole S5 chunks"
    assert (n_meta + seq) % SSM_CHUNK == 0, "the prompt state is read at a chunk boundary"
    ds_tile = -(-ds // (8 * SSM_CHUNK)) * 8 * SSM_CHUNK
    t_real = n_meta + seq
    t_len = -(-t_real // ROW_ALIGN) * ROW_ALIGN
    pad = t_len - t_real

    meta = jnp.broadcast_to(meta_tokens[None].astype(F32), (b, n_meta, d))
    h_p = jnp.concatenate([meta, x_prompt, jnp.zeros((b, pad, d), F32)], axis=1).reshape(b * t_len, d)
    h_s = x_sample.reshape(db * ds, d)
    zero_state = jnp.zeros((b, d // GROUP_CH, SSM_STATE), F32)

    outs = {name: [] for name in ("re_p", "im_p", "re_s", "im_s", "k_s", "v_s")}
    kv_p = (jnp.zeros((depth // 2, b, t_real * nh, HEAD_DIM), F32),) * 2
    i_ssm = i_sb = 0
    for layer in range(depth):
        if layer % 2 == 0:
            params = _ssm_params(ssm_a_re[i_ssm], ssm_a_im[i_ssm], ssm_log_dt[i_ssm], ssm_b_re[i_ssm],
                                 ssm_b_im[i_ssm], ssm_c_re[i_ssm], ssm_c_im[i_ssm])
            w_glu = ssm_w_glu[i_ssm].astype(BF16)
            u_p = _rmsnorm(h_p, norm_mix[layer], F32)
            y_p, re_p, im_p = _ssm_scan(u_p.reshape(b, t_len, d), params, zero_state, zero_state, t_real // SSM_CHUNK)
            h_p = _glu(y_p.reshape(b * t_len, d), u_p, h_p, ssm_d[i_ssm], w_glu)
            u_s = _rmsnorm(h_s, norm_mix[layer], F32)
            u_s_tile = jnp.pad(u_s.reshape(db, ds, d), ((0, 0), (0, ds_tile - ds), (0, 0)))
            y_s, re_s, im_s = _ssm_scan(u_s_tile, params, state_ssm_re[i_ssm], state_ssm_im[i_ssm], ds // SSM_CHUNK)
            h_s = _glu(y_s[:, :ds].reshape(db * ds, d), u_s, h_s, ssm_d[i_ssm], w_glu)
            outs["re_p"].append(re_p)
            outs["im_p"].append(im_p)
            outs["re_s"].append(re_s)
            outs["im_s"].append(im_s)
            i_ssm += 1
        else:
            w_qkv = sb_w_qkv[i_sb].astype(BF16)
            w_o = sb_w_o[i_sb].astype(BF16)
            gains = jnp.stack([sb_q_norm[i_sb], sb_k_norm[i_sb], jnp.ones_like(sb_q_norm[i_sb])]).reshape(3, 1, HEAD_DIM)
            qkv_p, kv_p = _qkv_prompt(h_p.reshape(b, t_len, d), norm_mix[layer], w_qkv, gains, t_real, i_sb, kv_p)
            o_p = _attn_prompt(qkv_p)
            h_p = _matmul_resid(o_p.reshape(b * t_len, d), w_o, h_p)
            qkv_s = _norm_matmul(h_s, norm_mix[layer], w_qkv, "qkv", F32, gains).reshape(db, ds, 3 * d)
            o_s = _attn_sample(qkv_s, cache_k[i_sb].reshape(db, past, d), cache_v[i_sb].reshape(db, past, d),
                               qkv_p[0, :n_meta, d:2 * d], qkv_p[0, :n_meta, 2 * d:])
            h_s = _matmul_resid(o_s.reshape(db * ds, d), w_o, h_s)
            outs["k_s"].append(qkv_s[:, :, d:2 * d].reshape(db, ds, nh, HEAD_DIM))
            outs["v_s"].append(qkv_s[:, :, 2 * d:].reshape(db, ds, nh, HEAD_DIM))
            i_sb += 1
        w_up = mlp_w_up[layer].astype(BF16)
        w_down = mlp_w_down[layer].astype(BF16)
        h_p = _matmul_resid(_norm_matmul(h_p, norm_mlp[layer], w_up, "relu2", BF16), w_down, h_p)
        h_s = _matmul_resid(_norm_matmul(h_s, norm_mlp[layer], w_up, "relu2", BF16), w_down, h_s)

    y_prompt = h_p.reshape(b, t_len, d)[:, n_meta:t_real]
    y_sample = h_s.reshape(db, ds, d)
    st = {k: jnp.stack(v) for k, v in outs.items()}
    st["k_p"], st["v_p"] = (a.reshape(depth // 2, b, t_real, nh, HEAD_DIM) for a in kv_p)
    return (y_prompt, y_sample, st["re_p"], st["im_p"], st["k_p"], st["v_p"],
            st["re_s"], st["im_s"], st["k_s"], st["v_s"])
```

```python
import functools
import math

import jax
import jax.numpy as jnp
from jax import lax
from jax.experimental import pallas as pl
from jax.experimental.pallas import tpu as pltpu

F32 = jnp.float32
BF16 = jnp.bfloat16

NORM_EPS = 1e-6
HEAD_DIM = 128
GROUP_CH = 16
SSM_STATE = 64
SSM_CHUNK = 16
SSM_BLOCK_GROUPS = 8
SSM_TILE_CHUNKS = 88
ATT_TK = 128
ATT_WINDOW = 3
SKIP_LOG = -100.0
ROW_ALIGN = 128
VMEM_LIMIT_BYTES = 56 * 1024 * 1024


def _cparams(*sem):
    return pltpu.CompilerParams(dimension_semantics=sem, vmem_limit_bytes=VMEM_LIMIT_BYTES)


def _pick(n, candidates):
    for c in candidates:
        if n % c == 0:
            return c
    raise ValueError(f"no tile in {candidates} divides {n}")


def _dot(a, b):
    return jnp.dot(a, b, preferred_element_type=F32)


def _dot_nt(a, b):
    return lax.dot_general(a, b, (((1,), (1,)), ((), ())), preferred_element_type=F32)


def _split_bf16(x):
    hi = x.astype(BF16)
    lo = (x - hi.astype(F32)).astype(BF16)
    return hi, lo


def _rms(x, gain):
    ms = jnp.mean(x * x, axis=-1, keepdims=True)
    return x * lax.rsqrt(ms + NORM_EPS) * gain


def _rmsnorm_kernel(x_ref, g_ref, o_ref):
    o_ref[...] = _rms(x_ref[...], g_ref[...]).astype(o_ref.dtype)


def _rmsnorm(x, gain, out_dtype):
    m, d = x.shape
    tm = _pick(m, (512, 256, 128))
    return pl.pallas_call(
        _rmsnorm_kernel,
        grid=(m // tm,),
        in_specs=[pl.BlockSpec((tm, d), lambda i: (i, 0)),
                  pl.BlockSpec((1, d), lambda i: (0, 0))],
        out_specs=pl.BlockSpec((tm, d), lambda i: (i, 0)),
        out_shape=jax.ShapeDtypeStruct((m, d), out_dtype),
        compiler_params=_cparams("parallel"),
        name="rmsnorm",
    )(x, gain.reshape(1, d))


def _norm_matmul_kernel(x_ref, g_ref, w_ref, *rest, epilogue, norm_tiles):
    if epilogue == "qkv":
        hg_ref, o_ref, xn_ref = rest
    else:
        o_ref, xn_ref = rest
    j = pl.program_id(1)

    @pl.when(j == 0)
    def _():
        xn_ref[...] = _rms(x_ref[...], g_ref[...]).astype(BF16)

    acc = _dot(xn_ref[...], w_ref[...])
    if epilogue == "relu2":
        o_ref[...] = jnp.square(jnp.maximum(acc, 0.0)).astype(o_ref.dtype)
    else:
        @pl.when(j < norm_tiles)
        def _():
            for hh in range(acc.shape[1] // HEAD_DIM):
                seg = acc[:, hh * HEAD_DIM:(hh + 1) * HEAD_DIM]
                o_ref[:, hh * HEAD_DIM:(hh + 1) * HEAD_DIM] = _rms(seg, hg_ref[0])

        @pl.when(j >= norm_tiles)
        def _():
            o_ref[...] = acc


def _norm_matmul(x, gain, w, epilogue, out_dtype, head_gains=None):
    m, d = x.shape
    n = w.shape[1]
    tm = _pick(m, (1024, 512, 256, 128))
    tn = _pick(d, (1024, 512, 256, 128)) if epilogue == "qkv" else _pick(n, (1024, 512, 256, 128))
    in_specs = [pl.BlockSpec((tm, d), lambda i, j: (i, 0)),
                pl.BlockSpec((1, d), lambda i, j: (0, 0)),
                pl.BlockSpec((d, tn), lambda i, j: (0, j))]
    args = [x, gain.reshape(1, d), w]
    norm_tiles = 0
    if epilogue == "qkv":
        tiles_per_part = d // tn
        norm_tiles = 2 * tiles_per_part
        in_specs.append(pl.BlockSpec((1, 1, HEAD_DIM), lambda i, j: (j // tiles_per_part, 0, 0)))
        args.append(head_gains)
    return pl.pallas_call(
        functools.partial(_norm_matmul_kernel, epilogue=epilogue, norm_tiles=norm_tiles),
        grid=(m // tm, n // tn),
        in_specs=in_specs,
        out_specs=pl.BlockSpec((tm, tn), lambda i, j: (i, j)),
        out_shape=jax.ShapeDtypeStruct((m, n), out_dtype),
        scratch_shapes=[pltpu.VMEM((tm, d), BF16)],
        compiler_params=_cparams("parallel", "arbitrary"),
        name="norm_matmul_" + epilogue,
    )(*args)


def _qkv_prompt_kernel(x_ref, g_ref, w_ref, hg_ref, k_in, v_in, o_ref, ko_hbm, vo_hbm, xn_ref, kbuf, vbuf, sem,
                       *, tiles_per_part, layer, row_tiles, t_real):
    del k_in, v_in
    bi = pl.program_id(0)
    i = pl.program_id(1)
    j = pl.program_id(2)

    @pl.when(j == 0)
    def _():
        xn_ref[...] = _rms(x_ref[0], g_ref[...]).astype(BF16)

    acc = _dot(xn_ref[...], w_ref[...])
    tm, tn = acc.shape
    heads_per_tile = tn // HEAD_DIM
    nheads = tiles_per_part * heads_per_tile
    rows_full = tm * nheads
    rows_last = (t_real - (row_tiles - 1) * tm) * nheads

    def for_tile_rows(fn):
        if rows_last == rows_full:
            fn(rows_full)
        else:
            pl.when(i < row_tiles - 1)(lambda: fn(rows_full))
            pl.when(i == row_tiles - 1)(lambda: fn(rows_last))

    def copy(buf, hbm, slot, rows):
        dst = hbm.at[layer, bi, pl.ds(pl.multiple_of(i * rows_full, rows_full), rows)]
        return pltpu.make_async_copy(buf.at[pl.ds(0, rows)], dst, sem.at[slot])

    for jj in range(3 * tiles_per_part):
        part = jj // tiles_per_part
        head0 = (jj % tiles_per_part) * heads_per_tile

        @pl.when(j == jj)
        def _(jj=jj, part=part, head0=head0):
            for hh in range(heads_per_tile):
                seg = acc[:, hh * HEAD_DIM:(hh + 1) * HEAD_DIM]
                if part < 2:
                    seg = _rms(seg, hg_ref[0])
                o_ref[0, :, hh * HEAD_DIM:(hh + 1) * HEAD_DIM] = seg.astype(o_ref.dtype)
                if part == 1:
                    kbuf[pl.ds(head0 + hh, tm, stride=nheads), :] = seg
                if part == 2:
                    vbuf[pl.ds(head0 + hh, tm, stride=nheads), :] = seg
            if jj == 2 * tiles_per_part - 1:
                for_tile_rows(lambda rows: copy(kbuf, ko_hbm, 0, rows).start())
            if jj == 3 * tiles_per_part - 1:
                def finish(rows):
                    copy(vbuf, vo_hbm, 1, rows).start()
                    copy(kbuf, ko_hbm, 0, rows).wait()
                    copy(vbuf, vo_hbm, 1, rows).wait()
                for_tile_rows(finish)


def _qkv_prompt(h, gain, w, head_gains, t_real, layer, kv_prev):
    b, t, d = h.shape
    nheads = d // HEAD_DIM
    tm = _pick(t, (704, 384, 256, 128))
    tn = _pick(d, (1024, 512, 256, 128))
    tiles_per_part = d // tn
    row_tiles = t // tm
    assert (row_tiles - 1) * tm < t_real <= t
    in_specs = [pl.BlockSpec((1, tm, d), lambda bi, i, j: (bi, i, 0)),
                pl.BlockSpec((1, d), lambda bi, i, j: (0, 0)),
                pl.BlockSpec((d, tn), lambda bi, i, j: (0, j)),
                pl.BlockSpec((1, 1, HEAD_DIM), lambda bi, i, j: (j // tiles_per_part, 0, 0)),
                pl.BlockSpec(memory_space=pl.ANY), pl.BlockSpec(memory_space=pl.ANY)]
    kv_shape = jax.ShapeDtypeStruct(kv_prev[0].shape, F32)
    assert kv_prev[0].shape[1:] == (b, t_real * nheads, HEAD_DIM)
    qkv, k_out, v_out = pl.pallas_call(
        functools.partial(_qkv_prompt_kernel, tiles_per_part=tiles_per_part, layer=layer,
                          row_tiles=row_tiles, t_real=t_real),
        grid=(b, row_tiles, 3 * tiles_per_part),
        in_specs=in_specs,
        out_specs=[pl.BlockSpec((1, tm, tn), lambda bi, i, j: (bi, i, j)),
                   pl.BlockSpec(memory_space=pl.ANY), pl.BlockSpec(memory_space=pl.ANY)],
        out_shape=[jax.ShapeDtypeStruct((b, t, 3 * d), BF16), kv_shape, kv_shape],
        scratch_shapes=[pltpu.VMEM((tm, d), BF16),
                        pltpu.VMEM((tm * nheads, HEAD_DIM), F32),
                        pltpu.VMEM((tm * nheads, HEAD_DIM), F32),
                        pltpu.SemaphoreType.DMA((2,))],
        input_output_aliases={4: 1, 5: 2},
        compiler_params=_cparams("parallel", "arbitrary", "arbitrary"),
        name="qkv_prompt",
    )(h, gain.reshape(1, d), w, head_gains, *kv_prev)
    return qkv, (k_out, v_out)


def _matmul_resid_kernel(x_ref, w_ref, r_ref, o_ref):
    @pl.when(pl.program_id(2) == 0)
    def _():
        o_ref[...] = r_ref[...]

    o_ref[...] += _dot(x_ref[...], w_ref[...])


def _matmul_resid(x, w, resid):
    m, k = x.shape
    n = w.shape[1]
    tm = _pick(m, (1024, 512, 256, 128))
    tn = _pick(n, (1024, 512, 256, 128))
    tk = _pick(k, (2048, 1024, 512, 256, 128))
    return pl.pallas_call(
        _matmul_resid_kernel,
        grid=(m // tm, n // tn, k // tk),
        in_specs=[pl.BlockSpec((tm, tk), lambda i, j, kk: (i, kk)),
                  pl.BlockSpec((tk, tn), lambda i, j, kk: (kk, j)),
                  pl.BlockSpec((tm, tn), lambda i, j, kk: (i, j))],
        out_specs=pl.BlockSpec((tm, tn), lambda i, j, kk: (i, j)),
        out_shape=jax.ShapeDtypeStruct((m, n), F32),
        compiler_params=_cparams("parallel", "parallel", "arbitrary"),
        name="matmul_resid",
    )(x, w, resid)


def _glu_kernel(y_ref, u_ref, h_ref, d_ref, w_ref, o_ref):
    g = jax.nn.gelu(y_ref[...] + d_ref[...] * u_ref[...])
    gate = jax.nn.sigmoid(_dot(g.astype(BF16), w_ref[...]))
    o_ref[...] = h_ref[...] + g * gate


def _glu(y, u, h, d_skip, w):
    m, d = y.shape
    tm = _pick(m, (256, 128))
    row = pl.BlockSpec((tm, d), lambda i: (i, 0))
    return pl.pallas_call(
        _glu_kernel,
        grid=(m // tm,),
        in_specs=[row, row, row,
                  pl.BlockSpec((1, d), lambda i: (0, 0)),
                  pl.BlockSpec((d, d), lambda i: (0, 0))],
        out_specs=row,
        out_shape=jax.ShapeDtypeStruct((m, d), F32),
        compiler_params=_cparams("parallel"),
        name="ssm_glu",
    )(y, u, h, d_skip.reshape(1, d), w)


def _ssm_kernel(x_ref, are_ref, aim_ref, ldt_ref, btr_ref, bti_ref, cr_ref, ci_ref, s0r_ref, s0i_ref,
                y_ref, sfr_ref, sfi_ref,
                pr_ref, pi_ref, tw_ref, wst_ref, et_ref, lhs_ref, v_ref, s_ref, cr_state, ci_state,
                *, nch_t, batch, final_chunk):
    L = SSM_CHUNK
    W = SSM_BLOCK_GROUPS * GROUP_CH
    PS = SSM_BLOCK_GROUPS * SSM_STATE
    nt = pl.program_id(1)

    @pl.when(nt == 0)
    def _():
        are = are_ref[0]
        aim = aim_ref[0]
        dt = jnp.exp(ldt_ref[0])
        mag = jnp.exp(are * dt)
        ang = aim * dt
        abr = mag * jnp.cos(ang)
        abi = mag * jnp.sin(ang)
        den = are * are + aim * aim
        nr = abr - 1.0
        f_re = (nr * are + abi * aim) / den
        f_im = (abi * are - nr * aim) / den

        pr = jnp.ones_like(abr)
        pi_ = jnp.zeros_like(abr)
        for k in range(L + 1):
            pr_ref[k:k + 1, :] = pr
            pi_ref[k:k + 1, :] = pi_
            pr, pi_ = pr * abr - pi_ * abi, pr * abi + pi_ * abr

        btr = btr_ref[0]
        bti = bti_ref[0]
        bbr = f_re * btr - f_im * bti
        bbi = f_re * bti + f_im * btr
        cr = cr_ref[0]
        ci = ci_ref[0]
        bb_hi, bb_lo = _split_bf16(jnp.concatenate([bbr, bbi], axis=1))
        tw_ref[(L - 1) * W:L * W, 0:W] = jnp.zeros((W, W), BF16)
        for k in range(L + 1):
            ar = pr_ref[k:k + 1, :]
            ai = pi_ref[k:k + 1, :]
            gk = jnp.concatenate([cr * ar - ci * ai, -(cr * ai + ci * ar)], axis=1)
            if k >= 1:
                et_ref[(k - 1) * W:k * W, :] = gk.astype(BF16)
            if k < L:
                wst_ref[(L - 1 - k) * W:(L - k) * W, :] = jnp.concatenate(
                    [bbr * ar - bbi * ai, bbr * ai + bbi * ar], axis=1).astype(BF16)
                g_hi, g_lo = _split_bf16(gk)
                bd = (_dot_nt(bb_hi, g_hi) + _dot_nt(bb_hi, g_lo) + _dot_nt(bb_lo, g_hi)).astype(BF16)
                for m in range(L):
                    for h in range(2):
                        if 2 * (L // 2 - 1 - m // 2) + h - m % 2 == k:
                            tw_ref[m * W:(m + 1) * W, h * W:(h + 1) * W] = bd
        cr_state[...] = s0r_ref[0]
        ci_state[...] = s0i_ref[0]

    for b in range(batch):
        for s in range(L):
            lhs_ref[b * nch_t:(b + 1) * nch_t, s * W:(s + 1) * W] = x_ref[b, pl.ds(s, nch_t, stride=L), :]

    nlb = 2 * PS // W
    for j in range(0, nlb, 2):
        v = _dot(lhs_ref[...].astype(BF16), wst_ref[:, j * W:(j + 2) * W])
        v_ref[j] = v[:, 0:W]
        v_ref[j + 1] = v[:, W:2 * W]

    alr = jnp.broadcast_to(pr_ref[L:L + 1, :], (batch, PS))
    ali = jnp.broadcast_to(pi_ref[L:L + 1, :], (batch, PS))
    sr = cr_state[...]
    si = ci_state[...]
    for n in range(nch_t):
        rows = pl.ds(n, batch, stride=nch_t)
        state = jnp.concatenate([sr, si], axis=1)
        for j in range(nlb):
            s_ref[j, rows, :] = state[:, j * W:(j + 1) * W]
        vn = jnp.concatenate([v_ref[j, rows, :] for j in range(nlb)], axis=1)
        sr, si = sr * alr - si * ali + vn[:, 0:PS], sr * ali + si * alr + vn[:, PS:2 * PS]
        if (final_chunk - 1) % nch_t == n:
            @pl.when(nt == (final_chunk - 1) // nch_t)
            def _(sr=sr, si=si):
                sfr_ref[0] = sr
                sfi_ref[0] = si
    cr_state[...] = sr
    ci_state[...] = si

    sb = jnp.concatenate([s_ref[j] for j in range(nlb)], axis=1).astype(BF16)
    for t2 in range(L // 2):
        kk = (t2 + 1) * 2 * W
        y2 = (_dot(lhs_ref[:, 0:kk].astype(BF16), tw_ref[(L // 2 - 1 - t2) * 2 * W:, :])
              + _dot_nt(sb, et_ref[t2 * 2 * W:(t2 + 1) * 2 * W, :]))
        for tl in range(2):
            for b in range(batch):
                y_ref[b, pl.ds(2 * t2 + tl, nch_t, stride=L), :] = y2[b * nch_t:(b + 1) * nch_t, tl * W:(tl + 1) * W]


def _block_lanes(p):
    g, pp = p.shape
    return p.reshape(g // SSM_BLOCK_GROUPS, 1, SSM_BLOCK_GROUPS * pp)


def _block_embed(t):
    g, c, pp = t.shape
    nb = g // SSM_BLOCK_GROUPS
    eye = jnp.eye(SSM_BLOCK_GROUPS, dtype=t.dtype)
    t = t.reshape(nb, SSM_BLOCK_GROUPS, c, 1, pp) * eye[None, :, None, :, None]
    return t.reshape(nb, SSM_BLOCK_GROUPS * c, SSM_BLOCK_GROUPS * pp)


def _ssm_params(a_re, a_im, log_dt, b_re, b_im, c_re, c_im):
    p = a_re.shape[1]
    return (_block_lanes(a_re), _block_lanes(a_im),
            _block_lanes(jnp.broadcast_to(log_dt[:, None], (log_dt.shape[0], p))),
            _block_embed(jnp.swapaxes(b_re, 1, 2)), _block_embed(jnp.swapaxes(b_im, 1, 2)),
            _block_embed(c_re), _block_embed(c_im))


def _ssm_scan(u, params, s0_re, s0_im, final_chunk):
    b, t, d = u.shape
    g = d // GROUP_CH
    L = SSM_CHUNK
    W = SSM_BLOCK_GROUPS * GROUP_CH
    PS = SSM_BLOCK_GROUPS * SSM_STATE
    nb = g // SSM_BLOCK_GROUPS
    nch = t // L
    nch_t = max(c for c in range(8, SSM_TILE_CHUNKS + 1, 8) if nch % c == 0)
    rows = b * nch_t
    s0r = s0_re.reshape(b, nb, PS).transpose(1, 0, 2)
    s0i = s0_im.reshape(b, nb, PS).transpose(1, 0, 2)
    vec = pl.BlockSpec((1, 1, PS), lambda i, j: (i, 0, 0))
    emb = pl.BlockSpec((1, W, PS), lambda i, j: (i, 0, 0))
    st = pl.BlockSpec((1, b, PS), lambda i, j: (i, 0, 0))
    xy = pl.BlockSpec((b, nch_t * L, W), lambda i, j: (0, j, i))
    y, sfr, sfi = pl.pallas_call(
        functools.partial(_ssm_kernel, nch_t=nch_t, batch=b, final_chunk=final_chunk),
        grid=(nb, nch // nch_t),
        in_specs=[xy, vec, vec, vec, emb, emb, emb, emb, st, st],
        out_specs=[xy, st, st],
        out_shape=[jax.ShapeDtypeStruct((b, t, d), F32),
                   jax.ShapeDtypeStruct((nb, b, PS), F32),
                   jax.ShapeDtypeStruct((nb, b, PS), F32)],
        scratch_shapes=[pltpu.VMEM((L + 8, PS), F32),
                        pltpu.VMEM((L + 8, PS), F32),
                        pltpu.VMEM((L * W, 2 * W), BF16),
                        pltpu.VMEM((L * W, 2 * PS), BF16),
                        pltpu.VMEM((L * W, 2 * PS), BF16),
                        pltpu.VMEM((rows, L * W), F32),
                        pltpu.VMEM((2 * PS // W, rows, W), F32),
                        pltpu.VMEM((2 * PS // W, rows, W), F32),
                        pltpu.VMEM((b, PS), F32),
                        pltpu.VMEM((b, PS), F32)],
        compiler_params=_cparams("parallel", "arbitrary"),
        name="ssm_scan",
    )(u, *params, s0r, s0i)
    sf_re = sfr.transpose(1, 0, 2).reshape(b, g, SSM_STATE)
    sf_im = sfi.transpose(1, 0, 2).reshape(b, g, SSM_STATE)
    return y, sf_re, sf_im


def _tri_ones(tk):
    j = jnp.arange(tk)[:, None]
    s = jnp.arange(tk)[None, :]
    return jnp.concatenate([(j > s).astype(BF16), jnp.ones((tk, HEAD_DIM), BF16)], axis=1)


def _sb_block(q, k, v, c, tri, mask):
    tq = q.shape[0]
    tk = k.shape[0]
    z = _dot_nt(q, k) * (HEAD_DIM ** -0.5)
    t = jnp.log(1.0 + jnp.exp(-jnp.abs(z)))
    log_keep = -jnp.maximum(z, 0.0) - t
    log_beta = jnp.minimum(z, 0.0) - t
    if mask is not None:
        log_keep = jnp.where(mask, log_keep, 0.0)
    hi, lo = _split_bf16(log_keep)
    s2 = _dot(jnp.concatenate([hi, lo], axis=0), tri)
    s = s2[:tq] + s2[tq:]
    after = s[:, :tk]
    row_sum = s[:, tk:]
    log_w = log_beta + after
    if c is not None:
        log_w = log_w + (c if tk == HEAD_DIM else jnp.concatenate([c] * (tk // HEAD_DIM), axis=1))
    w = jnp.exp(log_w)
    if mask is not None:
        w = jnp.where(mask, w, 0.0)
    return _dot(w.astype(BF16), v), row_sum


def _attn_prompt_kernel(q_ref, k_ref, v_ref, triw_ref, tri_ref, o_ref, kb_ref, vb_ref, *, unroll):
    t_len = q_ref.shape[1]
    tq = tk = ATT_TK
    wlen = ATT_WINDOW * tk
    front = wlen - tq
    zeros = jnp.zeros((front, HEAD_DIM), BF16)
    kb_ref[0:front, :] = zeros
    vb_ref[0:front, :] = zeros
    kb_ref[front:front + t_len, :] = k_ref[0]
    vb_ref[front:front + t_len, :] = v_ref[0]
    triw = triw_ref[...]

    def tail(qi, q, acc, c):
        tri = tri_ref[...]

        def cond(state):
            kj, _, _, cmax = state
            return jnp.logical_and(kj >= 0, cmax > SKIP_LOG)

        def body(state):
            kj, acc, c, _ = state
            k0 = pl.multiple_of(kj * tk, tk)
            pv, rs = _sb_block(q, kb_ref[pl.ds(k0 + front, tk), :], vb_ref[pl.ds(k0 + front, tk), :],
                               c, tri, None)
            c = c + rs
            return kj - 1, acc + pv, c, jnp.max(c)

        return lax.while_loop(cond, body, (qi - ATT_WINDOW, acc, c, jnp.max(c)))[1]

    def windows(qis):
        q0s = [pl.multiple_of(qi * tq, tq) for qi in qis]
        qs = [q_ref[0, pl.ds(q0, tq), :] for q0 in q0s]
        masks = []
        for q0 in q0s:
            qpos = q0 + lax.broadcasted_iota(jnp.int32, (tq, wlen), 0)
            kpos = q0 - front + lax.broadcasted_iota(jnp.int32, (tq, wlen), 1)
            masks.append(jnp.logical_and(kpos < qpos, kpos >= 0))
        zs = [_dot_nt(q, kb_ref[pl.ds(q0, wlen), :]) * (HEAD_DIM ** -0.5) for q, q0 in zip(qs, q0s)]
        ts = [jnp.log(1.0 + jnp.exp(-jnp.abs(z))) for z in zs]
        lks = [jnp.where(m, -jnp.maximum(z, 0.0) - t, 0.0) for z, t, m in zip(zs, ts, masks)]
        lbs = [jnp.minimum(z, 0.0) - t for z, t in zip(zs, ts)]
        s2s = [_dot(jnp.concatenate(_split_bf16(lk), axis=0), triw) for lk in lks]
        ss = [s2[:tq] + s2[tq:] for s2 in s2s]
        ws = [jnp.where(m, jnp.exp(lb + s[:, :wlen]), 0.0) for lb, s, m in zip(lbs, ss, masks)]
        pvs = [_dot(w.astype(BF16), vb_ref[pl.ds(q0, wlen), :]) for w, q0 in zip(ws, q0s)]
        return [(q0, q, pv, s[:, wlen:]) for q0, q, pv, s in zip(q0s, qs, pvs, ss)]

    def q_step(i, carry):
        blocks = windows([i * unroll + u for u in range(unroll)])
        cmax = functools.reduce(jnp.maximum, [jnp.max(blk[3]) for blk in blocks])
        for u, (q0, q, pv, c) in enumerate(blocks):
            acc = lax.cond(cmax > SKIP_LOG, functools.partial(tail, i * unroll + u), lambda q, pv, c: pv, q, pv, c)
            o_ref[0, pl.ds(q0, tq), :] = acc.astype(o_ref.dtype)
        return carry

    lax.fori_loop(0, t_len // (tq * unroll), q_step, 0)


def _attn_prompt(qkv):
    b, t, d3 = qkv.shape
    d = d3 // 3
    nh = d // HEAD_DIM
    unroll = _pick(t // ATT_TK, (11, 3, 2, 1))
    wlen = ATT_WINDOW * ATT_TK
    col = lambda off: pl.BlockSpec((1, t, HEAD_DIM), lambda bi, hi: (bi, 0, off + hi))
    const = lambda tk: pl.BlockSpec((tk, tk + HEAD_DIM), lambda bi, hi: (0, 0))
    return pl.pallas_call(
        functools.partial(_attn_prompt_kernel, unroll=unroll),
        grid=(b, nh),
        in_specs=[col(0), col(nh), col(2 * nh), const(wlen), const(ATT_TK)],
        out_specs=pl.BlockSpec((1, t, HEAD_DIM), lambda bi, hi: (bi, 0, hi)),
        out_shape=jax.ShapeDtypeStruct((b, t, d), BF16),
        scratch_shapes=[pltpu.VMEM((t + wlen - ATT_TK, HEAD_DIM), BF16),
                        pltpu.VMEM((t + wlen - ATT_TK, HEAD_DIM), BF16)],
        compiler_params=_cparams("parallel", "parallel"),
        name="attn_prompt",
    )(qkv, qkv, qkv, _tri_ones(wlen), _tri_ones(ATT_TK))


def _attn_sample_kernel(q_ref, kn_ref, vn_ref, km_ref, vm_ref, tri_ref, ck_hbm, cv_hbm, o_ref,
                        kc, vc, acc_ref, c_ref, sem, *, layer, n_meta):
    bi = pl.program_id(0)
    tq = q_ref.shape[1]
    tk = ATT_TK
    nheads = q_ref.shape[2] // HEAD_DIM
    block_rows = tk * nheads
    tri = tri_ref[...]
    qidx = lax.broadcasted_iota(jnp.int32, (tq, tk), 0)
    kidx = lax.broadcasted_iota(jnp.int32, (tq, tk), 1)
    cols = lambda h: slice(h * HEAD_DIM, (h + 1) * HEAD_DIM)
    q_head = lambda h: q_ref[0, :, cols(h)].astype(BF16)

    cmax = None
    for h in range(nheads):
        pv, c = _sb_block(q_head(h), kn_ref[0, :, cols(h)].astype(BF16), vn_ref[0, :, cols(h)].astype(BF16),
                          None, tri, kidx < qidx)
        acc_ref[h] = pv
        c_ref[h] = c
        cmax = jnp.max(c) if cmax is None else jnp.maximum(cmax, jnp.max(c))

    def cond(state):
        kj, cmax = state
        return jnp.logical_and(kj >= 0, cmax > SKIP_LOG)

    def body(state):
        kj, _ = state
        rows = pl.ds(pl.multiple_of(kj * block_rows, block_rows), block_rows)
        copy_k = pltpu.make_async_copy(ck_hbm.at[layer, bi, rows], kc, sem.at[0])
        copy_v = pltpu.make_async_copy(cv_hbm.at[layer, bi, rows], vc, sem.at[1])
        copy_k.start()
        copy_v.start()
        copy_k.wait()
        copy_v.wait()
        cmax = None
        for h in range(nheads):
            head_rows = pl.ds(h, tk, stride=nheads)
            pv, rs = _sb_block(q_head(h), kc[head_rows, :].astype(BF16), vc[head_rows, :].astype(BF16),
                               c_ref[h], tri, None)
            acc_ref[h] += pv
            c = c_ref[h] + rs
            c_ref[h] = c
            cmax = jnp.max(c) if cmax is None else jnp.maximum(cmax, jnp.max(c))
        return kj - 1, cmax

    _, cmax = lax.while_loop(cond, body, (ck_hbm.shape[2] // block_rows - 1, cmax))

    @pl.when(cmax > SKIP_LOG)
    def _():
        for h in range(nheads):
            pv, _ = _sb_block(q_head(h), km_ref[:, cols(h)].astype(BF16), vm_ref[:, cols(h)].astype(BF16),
                              c_ref[h], tri, kidx < n_meta)
            acc_ref[h] += pv

    for h in range(nheads):
        o_ref[0, :, cols(h)] = acc_ref[h].astype(o_ref.dtype)


def _attn_sample(qkv, cache_k, cache_v, layer, meta_k, meta_v):
    b, s, d3 = qkv.shape
    d = d3 // 3
    nh = d // HEAD_DIM
    past = cache_k.shape[2]
    n_meta = meta_k.shape[0]
    assert s <= ATT_TK and n_meta <= ATT_TK and past % ATT_TK == 0
    k_new = jnp.pad(qkv[:, :, d:2 * d], ((0, 0), (0, ATT_TK - s), (0, 0)))
    v_new = jnp.pad(qkv[:, :, 2 * d:], ((0, 0), (0, ATT_TK - s), (0, 0)))
    meta_k = jnp.pad(meta_k, ((0, ATT_TK - n_meta), (0, 0)))
    meta_v = jnp.pad(meta_v, ((0, ATT_TK - n_meta), (0, 0)))
    cache_rows = lambda c: c.reshape(c.shape[0], b, past * nh, HEAD_DIM)
    stream = lambda rows: pl.BlockSpec((1, rows, d), lambda bi: (bi, 0, 0))
    whole = lambda shape: pl.BlockSpec(shape, lambda bi: (0, 0))
    hbm = pl.BlockSpec(memory_space=pl.ANY)
    return pl.pallas_call(
        functools.partial(_attn_sample_kernel, layer=layer, n_meta=n_meta),
        grid=(b,),
        in_specs=[stream(s), stream(ATT_TK), stream(ATT_TK), whole((ATT_TK, d)), whole((ATT_TK, d)),
                  whole((ATT_TK, ATT_TK + HEAD_DIM)), hbm, hbm],
        out_specs=stream(s),
        out_shape=jax.ShapeDtypeStruct((b, s, d), BF16),
        scratch_shapes=[pltpu.VMEM((ATT_TK * nh, HEAD_DIM), F32),
                        pltpu.VMEM((ATT_TK * nh, HEAD_DIM), F32),
                        pltpu.VMEM((nh, s, HEAD_DIM), F32),
                        pltpu.VMEM((nh, s, HEAD_DIM), F32),
                        pltpu.SemaphoreType.DMA((2,))],
        compiler_params=_cparams("arbitrary"),
        name="attn_sample",
    )(qkv, k_new, v_new, meta_k, meta_v, _tri_ones(ATT_TK), cache_rows(cache_k), cache_rows(cache_v))


def kernel(x_prompt, x_sample, state_ssm_re, state_ssm_im, cache_k, cache_v, meta_tokens, norm_mix, norm_mlp, ssm_a_re, ssm_a_im, ssm_log_dt, ssm_b_re, ssm_b_im, ssm_c_re, ssm_c_im, ssm_d, ssm_w_glu, sb_w_qkv, sb_q_norm, sb_k_norm, sb_w_o, mlp_w_up, mlp_w_down):
    b, seq, d = x_prompt.shape
    db, ds, _ = x_sample.shape
    depth = norm_mix.shape[0]
    n_meta = meta_tokens.shape[0]
    nh = d // HEAD_DIM
    assert ds % SSM_CHUNK == 0, "the running streams advance by whole S5 chunks"
    assert (n_meta + seq) % SSM_CHUNK == 0, "the prompt state is read at a chunk boundary"
    ds_tile = -(-ds // (8 * SSM_CHUNK)) * 8 * SSM_CHUNK
    t_real = n_meta + seq
    t_len = -(-t_real // ROW_ALIGN) * ROW_ALIGN
    pad = t_len - t_real

    meta = jnp.broadcast_to(meta_tokens[None].astype(F32), (b, n_meta, d))
    h_p = jnp.concatenate([meta, x_prompt, jnp.zeros((b, pad, d), F32)], axis=1).reshape(b * t_len, d)
    h_s = x_sample.reshape(db * ds, d)
    zero_state = jnp.zeros((b, d // GROUP_CH, SSM_STATE), F32)

    outs = {name: [] for name in ("re_p", "im_p", "re_s", "im_s", "k_s", "v_s")}
    kv_p = (jnp.zeros((depth // 2, b, t_real * nh, HEAD_DIM), F32),) * 2
    i_ssm = i_sb = 0
    for layer in range(depth):
        if layer % 2 == 0:
            params = _ssm_params(ssm_a_re[i_ssm], ssm_a_im[i_ssm], ssm_log_dt[i_ssm], ssm_b_re[i_ssm],
                                 ssm_b_im[i_ssm], ssm_c_re[i_ssm], ssm_c_im[i_ssm])
            w_glu = ssm_w_glu[i_ssm].astype(BF16)
            u_p = _rmsnorm(h_p, norm_mix[layer], F32)
            y_p, re_p, im_p = _ssm_scan(u_p.reshape(b, t_len, d), params, zero_state, zero_state, t_real // SSM_CHUNK)
            h_p = _glu(y_p.reshape(b * t_len, d), u_p, h_p, ssm_d[i_ssm], w_glu)
            u_s = _rmsnorm(h_s, norm_mix[layer], F32)
            u_s_tile = jnp.pad(u_s.reshape(db, ds, d), ((0, 0), (0, ds_tile - ds), (0, 0)))
            y_s, re_s, im_s = _ssm_scan(u_s_tile, params, state_ssm_re[i_ssm], state_ssm_im[i_ssm], ds // SSM_CHUNK)
            h_s = _glu(y_s[:, :ds].reshape(db * ds, d), u_s, h_s, ssm_d[i_ssm], w_glu)
            outs["re_p"].append(re_p)
            outs["im_p"].append(im_p)
            outs["re_s"].append(re_s)
            outs["im_s"].append(im_s)
            i_ssm += 1
        else:
            w_qkv = sb_w_qkv[i_sb].astype(BF16)
            w_o = sb_w_o[i_sb].astype(BF16)
            gains = jnp.stack([sb_q_norm[i_sb], sb_k_norm[i_sb], jnp.ones_like(sb_q_norm[i_sb])]).reshape(3, 1, HEAD_DIM)
            qkv_p, kv_p = _qkv_prompt(h_p.reshape(b, t_len, d), norm_mix[layer], w_qkv, gains, t_real, i_sb, kv_p)
            o_p = _attn_prompt(qkv_p)
            h_p = _matmul_resid(o_p.reshape(b * t_len, d), w_o, h_p)
            qkv_s = _norm_matmul(h_s, norm_mix[layer], w_qkv, "qkv", F32, gains).reshape(db, ds, 3 * d)
            o_s = _attn_sample(qkv_s, cache_k, cache_v, i_sb, qkv_p[0, :n_meta, d:2 * d], qkv_p[0, :n_meta, 2 * d:])
            h_s = _matmul_resid(o_s.reshape(db * ds, d), w_o, h_s)
            outs["k_s"].append(qkv_s[:, :, d:2 * d].reshape(db, ds, nh, HEAD_DIM))
            outs["v_s"].append(qkv_s[:, :, 2 * d:].reshape(db, ds, nh, HEAD_DIM))
            i_sb += 1
        w_up = mlp_w_up[layer].astype(BF16)
        w_down = mlp_w_down[layer].astype(BF16)
        h_p = _matmul_resid(_norm_matmul(h_p, norm_mlp[layer], w_up, "relu2", BF16), w_down, h_p)
        h_s = _matmul_resid(_norm_matmul(h_s, norm_mlp[layer], w_up, "relu2", BF16), w_down, h_s)

    y_prompt = h_p.reshape(b, t_len, d)[:, n_meta:t_real]
    y_sample = h_s.reshape(db, ds, d)
    st = {k: jnp.stack(v) for k, v in outs.items()}
    st["k_p"], st["v_p"] = (a.reshape(depth // 2, b, t_real, nh, HEAD_DIM) for a in kv_p)
    return (y_prompt, y_sample, st["re_p"], st["im_p"], st["k_p"], st["v_p"],
            st["re_s"], st["im_s"], st["k_s"], st["v_s"])
```

```python
import functools
import math

import jax
import jax.numpy as jnp
from jax import lax
from jax.experimental import pallas as pl
from jax.experimental.pallas import tpu as pltpu

F32 = jnp.float32
BF16 = jnp.bfloat16

NORM_EPS = 1e-6
HEAD_DIM = 128
GROUP_CH = 16
SSM_STATE = 64
SSM_CHUNK = 16
SSM_BLOCK_GROUPS = 8
SSM_TILE_CHUNKS = 88
ATT_TK = 128
ATT_WINDOW = 3
SKIP_LOG = -100.0
ROW_ALIGN = 128
VMEM_LIMIT_BYTES = 56 * 1024 * 1024


def _cparams(*sem):
    return pltpu.CompilerParams(dimension_semantics=sem, vmem_limit_bytes=VMEM_LIMIT_BYTES)


def _pick(n, candidates):
    for c in candidates:
        if n % c == 0:
            return c
    raise ValueError(f"no tile in {candidates} divides {n}")


def _dot(a, b):
    return jnp.dot(a, b, preferred_element_type=F32)


def _dot_nt(a, b):
    return lax.dot_general(a, b, (((1,), (1,)), ((), ())), preferred_element_type=F32)


def _split_bf16(x):
    hi = x.astype(BF16)
    lo = (x - hi.astype(F32)).astype(BF16)
    return hi, lo


def _rms(x, gain):
    ms = jnp.mean(x * x, axis=-1, keepdims=True)
    return x * lax.rsqrt(ms + NORM_EPS) * gain


def _rmsnorm_kernel(x_ref, g_ref, o_ref):
    o_ref[...] = _rms(x_ref[...], g_ref[...]).astype(o_ref.dtype)


def _rmsnorm(x, gain, out_dtype):
    m, d = x.shape
    tm = _pick(m, (512, 256, 128))
    return pl.pallas_call(
        _rmsnorm_kernel,
        grid=(m // tm,),
        in_specs=[pl.BlockSpec((tm, d), lambda i: (i, 0)),
                  pl.BlockSpec((1, d), lambda i: (0, 0))],
        out_specs=pl.BlockSpec((tm, d), lambda i: (i, 0)),
        out_shape=jax.ShapeDtypeStruct((m, d), out_dtype),
        compiler_params=_cparams("parallel"),
        name="rmsnorm",
    )(x, gain.reshape(1, d))


def _norm_matmul_kernel(x_ref, g_ref, w_ref, *rest, epilogue, norm_tiles):
    if epilogue == "qkv":
        hg_ref, o_ref, xn_ref = rest
    else:
        o_ref, xn_ref = rest
    j = pl.program_id(1)

    @pl.when(j == 0)
    def _():
        xn_ref[...] = _rms(x_ref[...], g_ref[...]).astype(BF16)

    acc = _dot(xn_ref[...], w_ref[...])
    if epilogue == "relu2":
        o_ref[...] = jnp.square(jnp.maximum(acc, 0.0)).astype(o_ref.dtype)
    else:
        @pl.when(j < norm_tiles)
        def _():
            for hh in range(acc.shape[1] // HEAD_DIM):
                seg = acc[:, hh * HEAD_DIM:(hh + 1) * HEAD_DIM]
                o_ref[:, hh * HEAD_DIM:(hh + 1) * HEAD_DIM] = _rms(seg, hg_ref[0])

        @pl.when(j >= norm_tiles)
        def _():
            o_ref[...] = acc


def _norm_matmul(x, gain, w, epilogue, out_dtype, head_gains=None):
    m, d = x.shape
    n = w.shape[1]
    tm = _pick(m, (1024, 512, 256, 128))
    tn = _pick(d, (1024, 512, 256, 128)) if epilogue == "qkv" else _pick(n, (1024, 512, 256, 128))
    in_specs = [pl.BlockSpec((tm, d), lambda i, j: (i, 0)),
                pl.BlockSpec((1, d), lambda i, j: (0, 0)),
                pl.BlockSpec((d, tn), lambda i, j: (0, j))]
    args = [x, gain.reshape(1, d), w]
    norm_tiles = 0
    if epilogue == "qkv":
        tiles_per_part = d // tn
        norm_tiles = 2 * tiles_per_part
        in_specs.append(pl.BlockSpec((1, 1, HEAD_DIM), lambda i, j: (j // tiles_per_part, 0, 0)))
        args.append(head_gains)
    return pl.pallas_call(
        functools.partial(_norm_matmul_kernel, epilogue=epilogue, norm_tiles=norm_tiles),
        grid=(m // tm, n // tn),
        in_specs=in_specs,
        out_specs=pl.BlockSpec((tm, tn), lambda i, j: (i, j)),
        out_shape=jax.ShapeDtypeStruct((m, n), out_dtype),
        scratch_shapes=[pltpu.VMEM((tm, d), BF16)],
        compiler_params=_cparams("parallel", "arbitrary"),
        name="norm_matmul_" + epilogue,
    )(*args)


def _qkv_prompt_kernel(x_ref, g_ref, w_ref, hg_ref, k_in, v_in, o_ref, ko_hbm, vo_hbm, xn_ref, kbuf, vbuf, sem,
                       *, tiles_per_part, layer, row_tiles, t_real):
    del k_in, v_in
    bi = pl.program_id(0)
    i = pl.program_id(1)
    j = pl.program_id(2)

    @pl.when(j == 0)
    def _():
        xn_ref[...] = _rms(x_ref[0], g_ref[...]).astype(BF16)

    acc = _dot(xn_ref[...], w_ref[...])
    tm, tn = acc.shape
    heads_per_tile = tn // HEAD_DIM
    nheads = tiles_per_part * heads_per_tile
    rows_full = tm * nheads
    rows_last = (t_real - (row_tiles - 1) * tm) * nheads

    def for_tile_rows(fn):
        if rows_last == rows_full:
            fn(rows_full)
        else:
            pl.when(i < row_tiles - 1)(lambda: fn(rows_full))
            pl.when(i == row_tiles - 1)(lambda: fn(rows_last))

    def copy(buf, hbm, slot, rows):
        dst = hbm.at[layer, bi, pl.ds(pl.multiple_of(i * rows_full, rows_full), rows)]
        return pltpu.make_async_copy(buf.at[pl.ds(0, rows)], dst, sem.at[slot])

    for jj in range(3 * tiles_per_part):
        part = jj // tiles_per_part
        head0 = (jj % tiles_per_part) * heads_per_tile

        @pl.when(j == jj)
        def _(jj=jj, part=part, head0=head0):
            for hh in range(heads_per_tile):
                seg = acc[:, hh * HEAD_DIM:(hh + 1) * HEAD_DIM]
                if part < 2:
                    seg = _rms(seg, hg_ref[0])
                o_ref[0, :, hh * HEAD_DIM:(hh + 1) * HEAD_DIM] = seg.astype(o_ref.dtype)
                if part == 1:
                    kbuf[pl.ds(head0 + hh, tm, stride=nheads), :] = seg
                if part == 2:
                    vbuf[pl.ds(head0 + hh, tm, stride=nheads), :] = seg
            if jj == 2 * tiles_per_part - 1:
                for_tile_rows(lambda rows: copy(kbuf, ko_hbm, 0, rows).start())
            if jj == 3 * tiles_per_part - 1:
                def finish(rows):
                    copy(vbuf, vo_hbm, 1, rows).start()
                    copy(kbuf, ko_hbm, 0, rows).wait()
                    copy(vbuf, vo_hbm, 1, rows).wait()
                for_tile_rows(finish)


def _qkv_prompt(h, gain, w, head_gains, t_real, layer, kv_prev):
    b, t, d = h.shape
    nheads = d // HEAD_DIM
    tm = _pick(t, (704, 384, 256, 128))
    tn = _pick(d, (1024, 512, 256, 128))
    tiles_per_part = d // tn
    row_tiles = t // tm
    assert (row_tiles - 1) * tm < t_real <= t
    in_specs = [pl.BlockSpec((1, tm, d), lambda bi, i, j: (bi, i, 0)),
                pl.BlockSpec((1, d), lambda bi, i, j: (0, 0)),
                pl.BlockSpec((d, tn), lambda bi, i, j: (0, j)),
                pl.BlockSpec((1, 1, HEAD_DIM), lambda bi, i, j: (j // tiles_per_part, 0, 0)),
                pl.BlockSpec(memory_space=pl.ANY), pl.BlockSpec(memory_space=pl.ANY)]
    kv_shape = jax.ShapeDtypeStruct(kv_prev[0].shape, F32)
    assert kv_prev[0].shape[1:] == (b, t_real * nheads, HEAD_DIM)
    qkv, k_out, v_out = pl.pallas_call(
        functools.partial(_qkv_prompt_kernel, tiles_per_part=tiles_per_part, layer=layer,
                          row_tiles=row_tiles, t_real=t_real),
        grid=(b, row_tiles, 3 * tiles_per_part),
        in_specs=in_specs,
        out_specs=[pl.BlockSpec((1, tm, tn), lambda bi, i, j: (bi, i, j)),
                   pl.BlockSpec(memory_space=pl.ANY), pl.BlockSpec(memory_space=pl.ANY)],
        out_shape=[jax.ShapeDtypeStruct((b, t, 3 * d), BF16), kv_shape, kv_shape],
        scratch_shapes=[pltpu.VMEM((tm, d), BF16),
                        pltpu.VMEM((tm * nheads, HEAD_DIM), F32),
                        pltpu.VMEM((tm * nheads, HEAD_DIM), F32),
                        pltpu.SemaphoreType.DMA((2,))],
        input_output_aliases={4: 1, 5: 2},
        compiler_params=_cparams("parallel", "arbitrary", "arbitrary"),
        name="qkv_prompt",
    )(h, gain.reshape(1, d), w, head_gains, *kv_prev)
    return qkv, (k_out, v_out)


def _matmul_resid_kernel(x_ref, w_ref, r_ref, o_ref):
    @pl.when(pl.program_id(2) == 0)
    def _():
        o_ref[...] = r_ref[...]

    o_ref[...] += _dot(x_ref[...], w_ref[...])


def _matmul_resid(x, w, resid):
    m, k = x.shape
    n = w.shape[1]
    tm = _pick(m, (1024, 512, 256, 128))
    tn = _pick(n, (1024, 512, 256, 128))
    tk = _pick(k, (2048, 1024, 512, 256, 128))
    return pl.pallas_call(
        _matmul_resid_kernel,
        grid=(m // tm, n // tn, k // tk),
        in_specs=[pl.BlockSpec((tm, tk), lambda i, j, kk: (i, kk)),
                  pl.BlockSpec((tk, tn), lambda i, j, kk: (kk, j)),
                  pl.BlockSpec((tm, tn), lambda i, j, kk: (i, j))],
        out_specs=pl.BlockSpec((tm, tn), lambda i, j, kk: (i, j)),
        out_shape=jax.ShapeDtypeStruct((m, n), F32),
        compiler_params=_cparams("parallel", "parallel", "arbitrary"),
        name="matmul_resid",
    )(x, w, resid)


def _glu_kernel(y_ref, u_ref, h_ref, d_ref, w_ref, o_ref):
    g = jax.nn.gelu(y_ref[...] + d_ref[...] * u_ref[...])
    gate = jax.nn.sigmoid(_dot(g.astype(BF16), w_ref[...]))
    o_ref[...] = h_ref[...] + g * gate


def _glu(y, u, h, d_skip, w):
    m, d = y.shape
    tm = _pick(m, (256, 128))
    row = pl.BlockSpec((tm, d), lambda i: (i, 0))
    return pl.pallas_call(
        _glu_kernel,
        grid=(m // tm,),
        in_specs=[row, row, row,
                  pl.BlockSpec((1, d), lambda i: (0, 0)),
                  pl.BlockSpec((d, d), lambda i: (0, 0))],
        out_specs=row,
        out_shape=jax.ShapeDtypeStruct((m, d), F32),
        compiler_params=_cparams("parallel"),
        name="ssm_glu",
    )(y, u, h, d_skip.reshape(1, d), w)


def _ssm_kernel(x_ref, are_ref, aim_ref, ldt_ref, btr_ref, bti_ref, cr_ref, ci_ref, s0r_ref, s0i_ref,
                y_ref, sfr_ref, sfi_ref,
                pr_ref, pi_ref, tw_ref, wst_ref, et_ref, lhs_ref, v_ref, s_ref, cr_state, ci_state,
                *, nch_t, batch, final_chunk):
    L = SSM_CHUNK
    W = SSM_BLOCK_GROUPS * GROUP_CH
    PS = SSM_BLOCK_GROUPS * SSM_STATE
    nt = pl.program_id(1)

    @pl.when(nt == 0)
    def _():
        are = are_ref[0]
        aim = aim_ref[0]
        dt = jnp.exp(ldt_ref[0])
        mag = jnp.exp(are * dt)
        ang = aim * dt
        abr = mag * jnp.cos(ang)
        abi = mag * jnp.sin(ang)
        den = are * are + aim * aim
        nr = abr - 1.0
        f_re = (nr * are + abi * aim) / den
        f_im = (abi * are - nr * aim) / den

        pr = jnp.ones_like(abr)
        pi_ = jnp.zeros_like(abr)
        for k in range(L + 1):
            pr_ref[k:k + 1, :] = pr
            pi_ref[k:k + 1, :] = pi_
            pr, pi_ = pr * abr - pi_ * abi, pr * abi + pi_ * abr

        btr = btr_ref[0]
        bti = bti_ref[0]
        bbr = f_re * btr - f_im * bti
        bbi = f_re * bti + f_im * btr
        cr = cr_ref[0]
        ci = ci_ref[0]
        bb_hi, bb_lo = _split_bf16(jnp.concatenate([bbr, bbi], axis=1))
        tw_ref[(L - 1) * W:L * W, 0:W] = jnp.zeros((W, W), BF16)
        for k in range(L + 1):
            ar = pr_ref[k:k + 1, :]
            ai = pi_ref[k:k + 1, :]
            gk = jnp.concatenate([cr * ar - ci * ai, -(cr * ai + ci * ar)], axis=1)
            if k >= 1:
                et_ref[(k - 1) * W:k * W, :] = gk.astype(BF16)
            if k < L:
                wst_ref[(L - 1 - k) * W:(L - k) * W, :] = jnp.concatenate(
                    [bbr * ar - bbi * ai, bbr * ai + bbi * ar], axis=1).astype(BF16)
                g_hi, g_lo = _split_bf16(gk)
                bd = (_dot_nt(bb_hi, g_hi) + _dot_nt(bb_hi, g_lo) + _dot_nt(bb_lo, g_hi)).astype(BF16)
                for m in range(L):
                    for h in range(2):
                        if 2 * (L // 2 - 1 - m // 2) + h - m % 2 == k:
                            tw_ref[m * W:(m + 1) * W, h * W:(h + 1) * W] = bd
        cr_state[...] = s0r_ref[0]
        ci_state[...] = s0i_ref[0]

    for b in range(batch):
        for s in range(L):
            lhs_ref[b * nch_t:(b + 1) * nch_t, s * W:(s + 1) * W] = x_ref[b, pl.ds(s, nch_t, stride=L), :]

    nlb = 2 * PS // W
    for j in range(0, nlb, 2):
        v = _dot(lhs_ref[...].astype(BF16), wst_ref[:, j * W:(j + 2) * W])
        v_ref[j] = v[:, 0:W]
        v_ref[j + 1] = v[:, W:2 * W]

    alr = jnp.broadcast_to(pr_ref[L:L + 1, :], (batch, PS))
    ali = jnp.broadcast_to(pi_ref[L:L + 1, :], (batch, PS))
    sr = cr_state[...]
    si = ci_state[...]
    for n in range(nch_t):
        rows = pl.ds(n, batch, stride=nch_t)
        state = jnp.concatenate([sr, si], axis=1)
        for j in range(nlb):
            s_ref[j, rows, :] = state[:, j * W:(j + 1) * W]
        vn = jnp.concatenate([v_ref[j, rows, :] for j in range(nlb)], axis=1)
        sr, si = sr * alr - si * ali + vn[:, 0:PS], sr * ali + si * alr + vn[:, PS:2 * PS]
        if (final_chunk - 1) % nch_t == n:
            @pl.when(nt == (final_chunk - 1) // nch_t)
            def _(sr=sr, si=si):
                sfr_ref[0] = sr
                sfi_ref[0] = si
    cr_state[...] = sr
    ci_state[...] = si

    sb = jnp.concatenate([s_ref[j] for j in range(nlb)], axis=1).astype(BF16)
    for t2 in range(L // 2):
        kk = (t2 + 1) * 2 * W
        y2 = (_dot(lhs_ref[:, 0:kk].astype(BF16), tw_ref[(L // 2 - 1 - t2) * 2 * W:, :])
              + _dot_nt(sb, et_ref[t2 * 2 * W:(t2 + 1) * 2 * W, :]))
        for tl in range(2):
            for b in range(batch):
                y_ref[b, pl.ds(2 * t2 + tl, nch_t, stride=L), :] = y2[b * nch_t:(b + 1) * nch_t, tl * W:(tl + 1) * W]


def _block_lanes(p):
    g, pp = p.shape
    return p.reshape(g // SSM_BLOCK_GROUPS, 1, SSM_BLOCK_GROUPS * pp)


def _block_embed(t):
    g, c, pp = t.shape
    nb = g // SSM_BLOCK_GROUPS
    eye = jnp.eye(SSM_BLOCK_GROUPS, dtype=t.dtype)
    t = t.reshape(nb, SSM_BLOCK_GROUPS, c, 1, pp) * eye[None, :, None, :, None]
    return t.reshape(nb, SSM_BLOCK_GROUPS * c, SSM_BLOCK_GROUPS * pp)


def _ssm_params(a_re, a_im, log_dt, b_re, b_im, c_re, c_im):
    p = a_re.shape[1]
    return (_block_lanes(a_re), _block_lanes(a_im),
            _block_lanes(jnp.broadcast_to(log_dt[:, None], (log_dt.shape[0], p))),
            _block_embed(jnp.swapaxes(b_re, 1, 2)), _block_embed(jnp.swapaxes(b_im, 1, 2)),
            _block_embed(c_re), _block_embed(c_im))


def _ssm_scan(u, params, s0_re, s0_im, final_chunk):
    b, t, d = u.shape
    g = d // GROUP_CH
    L = SSM_CHUNK
    W = SSM_BLOCK_GROUPS * GROUP_CH
    PS = SSM_BLOCK_GROUPS * SSM_STATE
    nb = g // SSM_BLOCK_GROUPS
    nch = t // L
    nch_t = max(c for c in range(8, SSM_TILE_CHUNKS + 1, 8) if nch % c == 0)
    rows = b * nch_t
    s0r = s0_re.reshape(b, nb, PS).transpose(1, 0, 2)
    s0i = s0_im.reshape(b, nb, PS).transpose(1, 0, 2)
    vec = pl.BlockSpec((1, 1, PS), lambda i, j: (i, 0, 0))
    emb = pl.BlockSpec((1, W, PS), lambda i, j: (i, 0, 0))
    st = pl.BlockSpec((1, b, PS), lambda i, j: (i, 0, 0))
    xy = pl.BlockSpec((b, nch_t * L, W), lambda i, j: (0, j, i))
    y, sfr, sfi = pl.pallas_call(
        functools.partial(_ssm_kernel, nch_t=nch_t, batch=b, final_chunk=final_chunk),
        grid=(nb, nch // nch_t),
        in_specs=[xy, vec, vec, vec, emb, emb, emb, emb, st, st],
        out_specs=[xy, st, st],
        out_shape=[jax.ShapeDtypeStruct((b, t, d), F32),
                   jax.ShapeDtypeStruct((nb, b, PS), F32),
                   jax.ShapeDtypeStruct((nb, b, PS), F32)],
        scratch_shapes=[pltpu.VMEM((L + 8, PS), F32),
                        pltpu.VMEM((L + 8, PS), F32),
                        pltpu.VMEM((L * W, 2 * W), BF16),
                        pltpu.VMEM((L * W, 2 * PS), BF16),
                        pltpu.VMEM((L * W, 2 * PS), BF16),
                        pltpu.VMEM((rows, L * W), F32),
                        pltpu.VMEM((2 * PS // W, rows, W), F32),
                        pltpu.VMEM((2 * PS // W, rows, W), F32),
                        pltpu.VMEM((b, PS), F32),
                        pltpu.VMEM((b, PS), F32)],
        compiler_params=_cparams("parallel", "arbitrary"),
        name="ssm_scan",
    )(u, *params, s0r, s0i)
    sf_re = sfr.transpose(1, 0, 2).reshape(b, g, SSM_STATE)
    sf_im = sfi.transpose(1, 0, 2).reshape(b, g, SSM_STATE)
    return y, sf_re, sf_im


def _tri_ones(tk):
    j = jnp.arange(tk)[:, None]
    s = jnp.arange(tk)[None, :]
    return jnp.concatenate([(j > s).astype(BF16), jnp.ones((tk, HEAD_DIM), BF16)], axis=1)


def _sb_block(q, k, v, c, tri, mask):
    tq = q.shape[0]
    tk = k.shape[0]
    z = _dot_nt(q, k) * (HEAD_DIM ** -0.5)
    t = jnp.log(1.0 + jnp.exp(-jnp.abs(z)))
    log_keep = -jnp.maximum(z, 0.0) - t
    log_beta = jnp.minimum(z, 0.0) - t
    if mask is not None:
        log_keep = jnp.where(mask, log_keep, 0.0)
    hi, lo = _split_bf16(log_keep)
    s2 = _dot(jnp.concatenate([hi, lo], axis=0), tri)
    s = s2[:tq] + s2[tq:]
    after = s[:, :tk]
    row_sum = s[:, tk:]
    log_w = log_beta + after
    if c is not None:
        log_w = log_w + (c if tk == HEAD_DIM else jnp.concatenate([c] * (tk // HEAD_DIM), axis=1))
    w = jnp.exp(log_w)
    if mask is not None:
        w = jnp.where(mask, w, 0.0)
    return _dot(w.astype(BF16), v), row_sum


def _attn_prompt_kernel(q_ref, k_ref, v_ref, triw_ref, tri_ref, o_ref, kb_ref, vb_ref, *, unroll):
    t_len = q_ref.shape[1]
    tq = tk = ATT_TK
    wlen = ATT_WINDOW * tk
    front = wlen - tq
    zeros = jnp.zeros((front, HEAD_DIM), BF16)
    kb_ref[0:front, :] = zeros
    vb_ref[0:front, :] = zeros
    kb_ref[front:front + t_len, :] = k_ref[0]
    vb_ref[front:front + t_len, :] = v_ref[0]
    triw = triw_ref[...]

    def tail(qi, q, acc, c):
        tri = tri_ref[...]

        def cond(state):
            kj, _, _, cmax = state
            return jnp.logical_and(kj >= 0, cmax > SKIP_LOG)

        def body(state):
            kj, acc, c, _ = state
            k0 = pl.multiple_of(kj * tk, tk)
            pv, rs = _sb_block(q, kb_ref[pl.ds(k0 + front, tk), :], vb_ref[pl.ds(k0 + front, tk), :],
                               c, tri, None)
            c = c + rs
            return kj - 1, acc + pv, c, jnp.max(c)

        return lax.while_loop(cond, body, (qi - ATT_WINDOW, acc, c, jnp.max(c)))[1]

    earlier = lax.broadcasted_iota(jnp.int32, (tq, tk), 1) < lax.broadcasted_iota(jnp.int32, (tq, tk), 0)

    def causal(x):
        return jnp.concatenate([x[:, :wlen - tk], jnp.where(earlier, x[:, wlen - tk:], 0.0)], axis=1)

    def windows(qis):
        q0s = [pl.multiple_of(qi * tq, tq) for qi in qis]
        qs = [q_ref[0, pl.ds(q0, tq), :] for q0 in q0s]
        zs = [_dot_nt(q, kb_ref[pl.ds(q0, wlen), :]) * (HEAD_DIM ** -0.5) for q, q0 in zip(qs, q0s)]
        ts = [jnp.log(1.0 + jnp.exp(-jnp.abs(z))) for z in zs]
        lks = [causal(-jnp.maximum(z, 0.0) - t) for z, t in zip(zs, ts)]
        lbs = [jnp.minimum(z, 0.0) - t for z, t in zip(zs, ts)]
        ss = [_dot(lk.astype(BF16), triw) for lk in lks]
        ws = [causal(jnp.exp(lb + s[:, :wlen])) for lb, s in zip(lbs, ss)]
        pvs = [_dot(w.astype(BF16), vb_ref[pl.ds(q0, wlen), :]) for w, q0 in zip(ws, q0s)]
        return [(q0, q, pv, s[:, wlen:]) for q0, q, pv, s in zip(q0s, qs, pvs, ss)]

    def q_step(i, carry):
        blocks = windows([i * unroll + u for u in range(unroll)])
        cmax = functools.reduce(jnp.maximum, [jnp.max(blk[3]) for blk in blocks])
        for u, (q0, q, pv, c) in enumerate(blocks):
            acc = lax.cond(cmax > SKIP_LOG, functools.partial(tail, i * unroll + u), lambda q, pv, c: pv, q, pv, c)
            o_ref[0, pl.ds(q0, tq), :] = acc.astype(o_ref.dtype)
        return carry

    lax.fori_loop(0, t_len // (tq * unroll), q_step, 0)


def _attn_prompt(qkv):
    b, t, d3 = qkv.shape
    d = d3 // 3
    nh = d // HEAD_DIM
    unroll = _pick(t // ATT_TK, (11, 3, 2, 1))
    wlen = ATT_WINDOW * ATT_TK
    col = lambda off: pl.BlockSpec((1, t, HEAD_DIM), lambda bi, hi: (bi, 0, off + hi))
    const = lambda tk: pl.BlockSpec((tk, tk + HEAD_DIM), lambda bi, hi: (0, 0))
    return pl.pallas_call(
        functools.partial(_attn_prompt_kernel, unroll=unroll),
        grid=(b, nh),
        in_specs=[col(0), col(nh), col(2 * nh), const(wlen), const(ATT_TK)],
        out_specs=pl.BlockSpec((1, t, HEAD_DIM), lambda bi, hi: (bi, 0, hi)),
        out_shape=jax.ShapeDtypeStruct((b, t, d), BF16),
        scratch_shapes=[pltpu.VMEM((t + wlen - ATT_TK, HEAD_DIM), BF16),
                        pltpu.VMEM((t + wlen - ATT_TK, HEAD_DIM), BF16)],
        compiler_params=_cparams("parallel", "parallel"),
        name="attn_prompt",
    )(qkv, qkv, qkv, _tri_ones(wlen), _tri_ones(ATT_TK))


def _attn_sample_kernel(q_ref, kn_ref, vn_ref, km_ref, vm_ref, tri_ref, ck_hbm, cv_hbm, o_ref,
                        kc, vc, acc_ref, c_ref, sem, *, layer, n_meta):
    bi = pl.program_id(0)
    tq = q_ref.shape[1]
    tk = ATT_TK
    nheads = q_ref.shape[2] // HEAD_DIM
    block_rows = tk * nheads
    tri = tri_ref[...]
    qidx = lax.broadcasted_iota(jnp.int32, (tq, tk), 0)
    kidx = lax.broadcasted_iota(jnp.int32, (tq, tk), 1)
    cols = lambda h: slice(h * HEAD_DIM, (h + 1) * HEAD_DIM)
    q_head = lambda h: q_ref[0, :, cols(h)].astype(BF16)

    cmax = None
    for h in range(nheads):
        pv, c = _sb_block(q_head(h), kn_ref[0, :, cols(h)].astype(BF16), vn_ref[0, :, cols(h)].astype(BF16),
                          None, tri, kidx < qidx)
        acc_ref[h] = pv
        c_ref[h] = c
        cmax = jnp.max(c) if cmax is None else jnp.maximum(cmax, jnp.max(c))

    def cond(state):
        kj, cmax = state
        return jnp.logical_and(kj >= 0, cmax > SKIP_LOG)

    def body(state):
        kj, _ = state
        rows = pl.ds(pl.multiple_of(kj * block_rows, block_rows), block_rows)
        copy_k = pltpu.make_async_copy(ck_hbm.at[layer, bi, rows], kc, sem.at[0])
        copy_v = pltpu.make_async_copy(cv_hbm.at[layer, bi, rows], vc, sem.at[1])
        copy_k.start()
        copy_v.start()
        copy_k.wait()
        copy_v.wait()
        cmax = None
        for h in range(nheads):
            head_rows = pl.ds(h, tk, stride=nheads)
            pv, rs = _sb_block(q_head(h), kc[head_rows, :].astype(BF16), vc[head_rows, :].astype(BF16),
                               c_ref[h], tri, None)
            acc_ref[h] += pv
            c = c_ref[h] + rs
            c_ref[h] = c
            cmax = jnp.max(c) if cmax is None else jnp.maximum(cmax, jnp.max(c))
        return kj - 1, cmax

    _, cmax = lax.while_loop(cond, body, (ck_hbm.shape[2] // block_rows - 1, cmax))

    @pl.when(cmax > SKIP_LOG)
    def _():
        for h in range(nheads):
            pv, _ = _sb_block(q_head(h), km_ref[:, cols(h)].astype(BF16), vm_ref[:, cols(h)].astype(BF16),
                              c_ref[h], tri, kidx < n_meta)
            acc_ref[h] += pv

    for h in range(nheads):
        o_ref[0, :, cols(h)] = acc_ref[h].astype(o_ref.dtype)


def _attn_sample(qkv, cache_k, cache_v, layer, meta_k, meta_v):
    b, s, d3 = qkv.shape
    d = d3 // 3
    nh = d // HEAD_DIM
    past = cache_k.shape[2]
    n_meta = meta_k.shape[0]
    assert s <= ATT_TK and n_meta <= ATT_TK and past % ATT_TK == 0
    k_new = jnp.pad(qkv[:, :, d:2 * d], ((0, 0), (0, ATT_TK - s), (0, 0)))
    v_new = jnp.pad(qkv[:, :, 2 * d:], ((0, 0), (0, ATT_TK - s), (0, 0)))
    meta_k = jnp.pad(meta_k, ((0, ATT_TK - n_meta), (0, 0)))
    meta_v = jnp.pad(meta_v, ((0, ATT_TK - n_meta), (0, 0)))
    cache_rows = lambda c: c.reshape(c.shape[0], b, past * nh, HEAD_DIM)
    stream = lambda rows: pl.BlockSpec((1, rows, d), lambda bi: (bi, 0, 0))
    whole = lambda shape: pl.BlockSpec(shape, lambda bi: (0, 0))
    hbm = pl.BlockSpec(memory_space=pl.ANY)
    return pl.pallas_call(
        functools.partial(_attn_sample_kernel, layer=layer, n_meta=n_meta),
        grid=(b,),
        in_specs=[stream(s), stream(ATT_TK), stream(ATT_TK), whole((ATT_TK, d)), whole((ATT_TK, d)),
                  whole((ATT_TK, ATT_TK + HEAD_DIM)), hbm, hbm],
        out_specs=stream(s),
        out_shape=jax.ShapeDtypeStruct((b, s, d), BF16),
        scratch_shapes=[pltpu.VMEM((ATT_TK * nh, HEAD_DIM), F32),
                        pltpu.VMEM((ATT_TK * nh, HEAD_DIM), F32),
                        pltpu.VMEM((nh, s, HEAD_DIM), F32),
                        pltpu.VMEM((nh, s, HEAD_DIM), F32),
                        pltpu.SemaphoreType.DMA((2,))],
        compiler_params=_cparams("arbitrary"),
        name="attn_sample",
    )(qkv, k_new, v_new, meta_k, meta_v, _tri_ones(ATT_TK), cache_rows(cache_k), cache_rows(cache_v))


def kernel(x_prompt, x_sample, state_ssm_re, state_ssm_im, cache_k, cache_v, meta_tokens, norm_mix, norm_mlp, ssm_a_re, ssm_a_im, ssm_log_dt, ssm_b_re, ssm_b_im, ssm_c_re, ssm_c_im, ssm_d, ssm_w_glu, sb_w_qkv, sb_q_norm, sb_k_norm, sb_w_o, mlp_w_up, mlp_w_down):
    b, seq, d = x_prompt.shape
    db, ds, _ = x_sample.shape
    depth = norm_mix.shape[0]
    n_meta = meta_tokens.shape[0]
    nh = d // HEAD_DIM
    assert ds % SSM_CHUNK == 0, "the running streams advance by whole S5 chunks"
    assert (n_meta + seq) % SSM_CHUNK == 0, "the prompt state is read at a chunk boundary"
    ds_tile = -(-ds // (8 * SSM_CHUNK)) * 8 * SSM_CHUNK
    t_real = n_meta + seq
    t_len = -(-t_real // ROW_ALIGN) * ROW_ALIGN
    pad = t_len - t_real

    meta = jnp.broadcast_to(meta_tokens[None].astype(F32), (b, n_meta, d))
    h_p = jnp.concatenate([meta, x_prompt, jnp.zeros((b, pad, d), F32)], axis=1).reshape(b * t_len, d)
    h_s = x_sample.reshape(db * ds, d)
    zero_state = jnp.zeros((b, d // GROUP_CH, SSM_STATE), F32)

    outs = {name: [] for name in ("re_p", "im_p", "re_s", "im_s", "k_s", "v_s")}
    kv_p = (jnp.zeros((depth // 2, b, t_real * nh, HEAD_DIM), F32),) * 2
    i_ssm = i_sb = 0
    for layer in range(depth):
        if layer % 2 == 0:
            params = _ssm_params(ssm_a_re[i_ssm], ssm_a_im[i_ssm], ssm_log_dt[i_ssm], ssm_b_re[i_ssm],
                                 ssm_b_im[i_ssm], ssm_c_re[i_ssm], ssm_c_im[i_ssm])
            w_glu = ssm_w_glu[i_ssm].astype(BF16)
            u_p = _rmsnorm(h_p, norm_mix[layer], F32)
            y_p, re_p, im_p = _ssm_scan(u_p.reshape(b, t_len, d), params, zero_state, zero_state, t_real // SSM_CHUNK)
            h_p = _glu(y_p.reshape(b * t_len, d), u_p, h_p, ssm_d[i_ssm], w_glu)
            u_s = _rmsnorm(h_s, norm_mix[layer], F32)
            u_s_tile = jnp.pad(u_s.reshape(db, ds, d), ((0, 0), (0, ds_tile - ds), (0, 0)))
            y_s, re_s, im_s = _ssm_scan(u_s_tile, params, state_ssm_re[i_ssm], state_ssm_im[i_ssm], ds // SSM_CHUNK)
            h_s = _glu(y_s[:, :ds].reshape(db * ds, d), u_s, h_s, ssm_d[i_ssm], w_glu)
            outs["re_p"].append(re_p)
            outs["im_p"].append(im_p)
            outs["re_s"].append(re_s)
            outs["im_s"].append(im_s)
            i_ssm += 1
        else:
            w_qkv = sb_w_qkv[i_sb].astype(BF16)
            w_o = sb_w_o[i_sb].astype(BF16)
            gains = jnp.stack([sb_q_norm[i_sb], sb_k_norm[i_sb], jnp.ones_like(sb_q_norm[i_sb])]).reshape(3, 1, HEAD_DIM)
            qkv_p, kv_p = _qkv_prompt(h_p.reshape(b, t_len, d), norm_mix[layer], w_qkv, gains, t_real, i_sb, kv_p)
            o_p = _attn_prompt(qkv_p)
            h_p = _matmul_resid(o_p.reshape(b * t_len, d), w_o, h_p)
            qkv_s = _norm_matmul(h_s, norm_mix[layer], w_qkv, "qkv", F32, gains).reshape(db, ds, 3 * d)
            o_s = _attn_sample(qkv_s, cache_k, cache_v, i_sb, qkv_p[0, :n_meta, d:2 * d], qkv_p[0, :n_meta, 2 * d:])
            h_s = _matmul_resid(o_s.reshape(db * ds, d), w_o, h_s)
            outs["k_s"].append(qkv_s[:, :, d:2 * d].reshape(db, ds, nh, HEAD_DIM))
            outs["v_s"].append(qkv_s[:, :, 2 * d:].reshape(db, ds, nh, HEAD_DIM))
            i_sb += 1
        w_up = mlp_w_up[layer].astype(BF16)
        w_down = mlp_w_down[layer].astype(BF16)
        h_p = _matmul_resid(_norm_matmul(h_p, norm_mlp[layer], w_up, "relu2", BF16), w_down, h_p)
        h_s = _matmul_resid(_norm_matmul(h_s, norm_mlp[layer], w_up, "relu2", BF16), w_down, h_s)

    y_prompt = h_p.reshape(b, t_len, d)[:, n_meta:t_real]
    y_sample = h_s.reshape(db, ds, d)
    st = {k: jnp.stack(v) for k, v in outs.items()}
    st["k_p"], st["v_p"] = (a.reshape(depth // 2, b, t_real, nh, HEAD_DIM) for a in kv_p)
    return (y_prompt, y_sample, st["re_p"], st["im_p"], st["k_p"], st["v_p"],
            st["re_s"], st["im_s"], st["k_s"], st["v_s"])
```
